```python
import jax, jax.numpy as jnp
from jax import lax
import numpy as np

D_MODEL = 2048
BATCH = 2
SEQ = 16384
DEPTH = 4
DEC_BATCH = 2
DEC_SEQ = 8192
PAST_LEN = 128

POOL_WIDTH = D_MODEL // 2
POOL_GROUPS = 4
POOL_GROUP_CH = POOL_WIDTH // POOL_GROUPS
POOL_WINDOWS = (2, 4, 8, 16)
POOL_OUT_CH = D_MODEL // POOL_GROUPS
CONV_WIDTH = D_MODEL // 2
CONV_KERNEL = 31
N_EXPERTS = 16
EXPERT_FF = D_MODEL
CAPACITY_FACTOR = 2
GATE_COLS = 2 * D_MODEL
IN_COLS = POOL_WIDTH + 2 * CONV_WIDTH + GATE_COLS
EPS = 1e-6

kernel_name = "hybrid_pool_conformer_ec_encoder"


def rms_norm(x, g):
    xf = x.astype(jnp.float32)
    y = xf * lax.rsqrt(jnp.mean(xf * xf, axis=-1, keepdims=True) + EPS)
    return (y * g.astype(jnp.float32)).astype(x.dtype)


def layer_norm(x, g, b):
    xf = x.astype(jnp.float32)
    mu = jnp.mean(xf, axis=-1, keepdims=True)
    xc = xf - mu
    var = jnp.mean(xc * xc, axis=-1, keepdims=True)
    y = xc * lax.rsqrt(var + EPS) * g.astype(jnp.float32) + b.astype(jnp.float32)
    return y.astype(x.dtype)


def centred_mean_minus_self(v, window):
    s = v.shape[1]
    left = window // 2
    right = window - 1 - left
    vf = v.astype(jnp.float32)
    cs = jnp.pad(jnp.cumsum(vf, axis=1), ((0, 0), (1, 0), (0, 0)))
    t = jnp.arange(s)
    lo = jnp.maximum(t - left, 0)
    hi = jnp.minimum(t + right, s - 1) + 1
    total = jnp.take(cs, hi, axis=1) - jnp.take(cs, lo, axis=1)
    count = (hi - lo).astype(jnp.float32)[None, :, None]
    return (total / count - vf).astype(v.dtype)


def pool_branch(u, w_pool, pool_scale):
    b, s, _ = u.shape
    groups = u.reshape(b, s, POOL_GROUPS, POOL_GROUP_CH)
    mixed = jnp.stack(
        [centred_mean_minus_self(groups[:, :, i], w) for i, w in enumerate(POOL_WINDOWS)],
        axis=2)
    y = jnp.einsum('bsgc,gcd->bsgd', mixed, w_pool).reshape(b, s, D_MODEL)
    return y * pool_scale


def conv_branch(u, dw_w, dw_b, cn_g, cn_b, w_conv_out):
    a, gate = jnp.split(u, 2, axis=-1)
    v = a * jax.nn.sigmoid(gate)
    pad = CONV_KERNEL // 2
    v = lax.conv_general_dilated(
        v, dw_w[:, None, :], window_strides=(1,), padding=[(pad, pad)],
        dimension_numbers=('NWC', 'WIO', 'NWC'),
        feature_group_count=CONV_WIDTH) + dw_b
    v = jax.nn.silu(layer_norm(v, cn_g, cn_b))
    return v @ w_conv_out


def expert_choice_ffn(h, w_router, w_gate, w_up, w_down):
    b, s, d = h.shape
    n_tok = b * s
    cap = max(1, CAPACITY_FACTOR * n_tok // N_EXPERTS)
    xt = h.reshape(n_tok, d)
    aff = jax.nn.softmax((xt @ w_router).astype(jnp.float32), axis=-1)
    gate, idx = lax.top_k(aff.T, cap)
    xe = jnp.take(xt, idx, axis=0)
    hid = jax.nn.silu(jnp.einsum('ecd,edf->ecf', xe, w_gate)) * jnp.einsum('ecd,edf->ecf', xe, w_up)
    ye = jnp.einsum('ecf,efd->ecd', hid, w_down) * gate[..., None].astype(h.dtype)
    out = jnp.zeros_like(xt).at[idx.reshape(-1)].add(ye.reshape(-1, d))
    return out.reshape(b, s, d)


def run_trunk(x, norm1_g, w_in, w_pool, pool_scale, dw_w, dw_b, cn_g, cn_b, w_conv_out,
              w_out, norm2_g, w_router, w_gate, w_up, w_down, final_g):
    for l in range(DEPTH):
        h = rms_norm(x, norm1_g[l])
        u = h @ w_in[l]
        u_pool, u_conv, u_gate = jnp.split(u, [POOL_WIDTH, POOL_WIDTH + 2 * CONV_WIDTH], axis=-1)
        y_pool = pool_branch(u_pool, w_pool[l], pool_scale[l])
        y_conv = conv_branch(u_conv, dw_w[l], dw_b[l], cn_g[l], cn_b[l], w_conv_out[l])
        g_pool, g_conv = jnp.split(jax.nn.sigmoid(u_gate), 2, axis=-1)
        x = x + (g_pool * y_pool + g_conv * y_conv) @ w_out[l]
        x = x + expert_choice_ffn(rms_norm(x, norm2_g[l]), w_router[l], w_gate[l], w_up[l], w_down[l])
    return rms_norm(x, final_g)


def setup_inputs(seed: int = 0) -> dict:
    key = jax.random.key(seed)
    ks = jax.random.split(key, 20)
    f32 = jnp.float32
    nrm = lambda k, shape, scale: jax.random.normal(k, shape, f32) * scale
    return {
        "x_prompt": nrm(ks[0], (BATCH, SEQ, D_MODEL), 1.0),
        "x_sample": nrm(ks[1], (DEC_BATCH, DEC_SEQ, D_MODEL), 1.0),
        "norm1_g": 1.0 + nrm(ks[2], (DEPTH, D_MODEL), 0.05),
        "w_in": nrm(ks[3], (DEPTH, D_MODEL, IN_COLS), D_MODEL ** -0.5),
        "w_pool": nrm(ks[4], (DEPTH, POOL_GROUPS, POOL_GROUP_CH, POOL_OUT_CH), POOL_GROUP_CH ** -0.5),
        "pool_scale": 1.0 + nrm(ks[5], (DEPTH, D_MODEL), 0.1),
        "dw_w": nrm(ks[6], (DEPTH, CONV_KERNEL, CONV_WIDTH), CONV_KERNEL ** -0.5),
        "dw_b": nrm(ks[7], (DEPTH, CONV_WIDTH), 0.02),
        "cn_g": 1.0 + nrm(ks[8], (DEPTH, CONV_WIDTH), 0.05),
        "cn_b": nrm(ks[9], (DEPTH, CONV_WIDTH), 0.02),
        "w_conv_out": nrm(ks[10], (DEPTH, CONV_WIDTH, D_MODEL), CONV_WIDTH ** -0.5),
        "w_out": nrm(ks[11], (DEPTH, D_MODEL, D_MODEL), D_MODEL ** -0.5),
        "norm2_g": 1.0 + nrm(ks[12], (DEPTH, D_MODEL), 0.05),
        "w_router": nrm(ks[13], (DEPTH, D_MODEL, N_EXPERTS), D_MODEL ** -0.5),
        "w_gate": nrm(ks[14], (DEPTH, N_EXPERTS, D_MODEL, EXPERT_FF), D_MODEL ** -0.5),
        "w_up": nrm(ks[15], (DEPTH, N_EXPERTS, D_MODEL, EXPERT_FF), D_MODEL ** -0.5),
        "w_down": nrm(ks[16], (DEPTH, N_EXPERTS, EXPERT_FF, D_MODEL), EXPERT_FF ** -0.5),
        "final_g": 1.0 + nrm(ks[17], (D_MODEL,), 0.05),
    }


def reference(x_prompt, x_sample, norm1_g, w_in, w_pool, pool_scale, dw_w, dw_b, cn_g, cn_b,
              w_conv_out, w_out, norm2_g, w_router, w_gate, w_up, w_down, final_g):
    y_prompt = run_trunk(x_prompt, norm1_g, w_in, w_pool, pool_scale, dw_w, dw_b, cn_g, cn_b,
                         w_conv_out, w_out, norm2_g, w_router, w_gate, w_up, w_down, final_g)
    y_sample = run_trunk(x_sample, norm1_g, w_in, w_pool, pool_scale, dw_w, dw_b, cn_g, cn_b,
                         w_conv_out, w_out, norm2_g, w_router, w_gate, w_up, w_down, final_g)
    return (y_prompt, y_sample)
```

```python
import functools

import jax
import jax.numpy as jnp
from jax import lax
from jax.experimental import pallas as pl
from jax.experimental.pallas import tpu as pltpu

EPS = 1e-6
POOL_WINDOWS = (2, 4, 8, 16)
CAPACITY_FACTOR = 2

LANES = 128
BF16_ROWS = 16
HALO = 16
SLOT_BLOCK = 64
TOKEN_TILE = 256
VMEM_LIMIT = 58 * 1024 * 1024
INT_MAX = 2**31 - 1

f32 = jnp.float32
bf16 = jnp.bfloat16
i32 = jnp.int32


def _rms(x, g):
    return x * lax.rsqrt(jnp.mean(x * x, axis=-1, keepdims=True) + EPS) * g


def _sigmoid(x):
    return 1.0 / (1.0 + jnp.exp(-x))


def _cparams(sem, limit=VMEM_LIMIT):
    return pltpu.CompilerParams(dimension_semantics=sem, vmem_limit_bytes=limit)


def _in_proj_body(x_ref, g_ref, w_ref, up_ref, v_ref, gate_ref, h_scr, *, n_pool, n_conv):
    j = pl.program_id(1)

    @pl.when(j == 0)
    def _():
        h_scr[...] = _rms(x_ref[...], g_ref[...]).astype(bf16)

    u = jnp.dot(h_scr[...], w_ref[...], preferred_element_type=f32)

    @pl.when(j < n_pool)
    def _():
        up_ref[...] = u

    @pl.when((j >= n_pool) & (j < n_pool + n_conv))
    def _():
        q = u.shape[1] // 2
        v_ref[...] = u[:, :q] * _sigmoid(u[:, q:])

    @pl.when(j >= n_pool + n_conv)
    def _():
        gate_ref[...] = _sigmoid(u).astype(bf16)


def _in_proj(x2, g1, w_perm, *, tm, tn):
    n, d = x2.shape
    n_pool, n_conv, n_gate = (d // 2) // tn, d // tn, (2 * d) // tn
    q = tn // 2
    body = functools.partial(_in_proj_body, n_pool=n_pool, n_conv=n_conv)
    return pl.pallas_call(
        body,
        grid=(n // tm, n_pool + n_conv + n_gate),
        in_specs=[
            pl.BlockSpec((tm, d), lambda i, j: (i, 0)),
            pl.BlockSpec((1, d), lambda i, j: (0, 0)),
            pl.BlockSpec((d, tn), lambda i, j: (0, j)),
        ],
        out_specs=[
            pl.BlockSpec((tm, tn), lambda i, j: (i, jnp.clip(j, 0, n_pool - 1))),
            pl.BlockSpec((tm, q), lambda i, j: (i, jnp.clip(j - n_pool, 0, n_conv - 1))),
            pl.BlockSpec((tm, tn), lambda i, j: (i, jnp.clip(j - n_pool - n_conv, 0, n_gate - 1))),
        ],
        out_shape=[
            jax.ShapeDtypeStruct((n, d // 2), f32),
            jax.ShapeDtypeStruct((n, d // 2), f32),
            jax.ShapeDtypeStruct((n, 2 * d), bf16),
        ],
        scratch_shapes=[pltpu.VMEM((tm, d), bf16)],
        compiler_params=_cparams(("arbitrary", "arbitrary")),
        name="in_proj",
    )(x2, g1, w_perm)


def _mixer_body(x_ref, upc_ref, upp_ref, upn_ref, vc_ref, vp_ref, vn_ref, gate_ref,
                wpool_ref, pscale_ref, dww_ref, dwb_ref, cng_ref, cnb_ref, wco_ref, wout_ref,
                n2g_ref, wr_ref, xmid_ref, aff_ref, afft_ref, upext, vext,
                *, t, seq, n_exp, conv_k):
    j = pl.program_id(0)
    s0 = (j * t) % seq
    has_prev = s0 > 0
    has_next = s0 + t < seq
    d = x_ref.shape[1]

    upext[0:HALO, :] = jnp.where(has_prev, upp_ref[...], 0.0)
    upext[HALO:HALO + t, :] = upc_ref[...]
    upext[HALO + t:, :] = jnp.where(has_next, upn_ref[...], 0.0)
    vext[0:HALO, :] = jnp.where(has_prev, vp_ref[...], 0.0)
    vext[HALO:HALO + t, :] = vc_ref[...]
    vext[HALO + t:, :] = jnp.where(has_next, vn_ref[...], 0.0)

    spos = s0 + lax.broadcasted_iota(i32, (t, 1), 0)
    gch = upext.shape[1] // len(POOL_WINDOWS)
    y_pool = []
    for gi, w in enumerate(POOL_WINDOWS):
        left = w // 2
        right = w - 1 - left
        cols = slice(gi * gch, (gi + 1) * gch)
        total = upext[HALO - left:HALO - left + t, cols]
        for dd in range(-left + 1, right + 1):
            total = total + upext[HALO + dd:HALO + dd + t, cols]
        lo = jnp.maximum(spos - left, 0)
        hi = jnp.minimum(spos + right, seq - 1) + 1
        count = (hi - lo).astype(f32)
        mixed = total / count - upext[HALO:HALO + t, cols]
        y_pool.append(jnp.dot(mixed.astype(bf16), wpool_ref[gi], preferred_element_type=f32))
    y_pool = jnp.concatenate(y_pool, axis=1) * pscale_ref[...]

    pad = conv_k // 2
    cw = vext.shape[1]
    conv = []
    for c0 in range(0, cw, LANES):
        cols = slice(c0, c0 + LANES)
        acc = jnp.zeros((t, LANES), f32) + dwb_ref[:, cols]
        for k in range(conv_k):
            off = HALO - pad + k
            acc = acc + dww_ref[k:k + 1, cols] * vext[off:off + t, cols]
        conv.append(acc)
    conv = jnp.concatenate(conv, axis=1)
    mu = jnp.mean(conv, axis=-1, keepdims=True)
    xc = conv - mu
    var = jnp.mean(xc * xc, axis=-1, keepdims=True)
    ln = xc * lax.rsqrt(var + EPS) * cng_ref[...] + cnb_ref[...]
    act = ln * _sigmoid(ln)
    y_conv = jnp.dot(act.astype(bf16), wco_ref[...], preferred_element_type=f32)

    merged = (gate_ref[:, :d].astype(f32) * y_pool + gate_ref[:, d:].astype(f32) * y_conv)
    x_mid = x_ref[...] + jnp.dot(merged.astype(bf16), wout_ref[...], preferred_element_type=f32)
    xmid_ref[...] = x_mid

    h2 = _rms(x_mid, n2g_ref[...])
    logits = jnp.dot(h2, wr_ref[...], preferred_element_type=f32, precision=lax.Precision.HIGHEST)
    lane = lax.broadcasted_iota(i32, logits.shape, 1)
    logits = jnp.where(lane < n_exp, logits, -1e30)
    ex = jnp.exp(logits - jnp.max(logits, axis=-1, keepdims=True))
    aff = ex / jnp.sum(ex, axis=-1, keepdims=True)
    aff_ref[...] = aff
    afft_ref[...] = aff.T[:n_exp, :]


def _const_spec(shape):
    zeros = (0,) * len(shape)
    return pl.BlockSpec(shape, lambda j: zeros, pipeline_mode=pl.Buffered(1))


def _mixer(x2, up, v, gates, wpool, pscale, dww, dwb, cng, cnb, wco, wout, n2g, wr_pad,
           *, t, seq, n_exp):
    n, d = x2.shape
    hw = d // 2
    nh = n // HALO
    th = t // HALO
    conv_k = dww.shape[0]
    cur = lambda j: (j, 0)
    prev = lambda j: (jnp.maximum(j * th - 1, 0), 0)
    nxt = lambda j: (jnp.minimum((j + 1) * th, nh - 1), 0)
    body = functools.partial(_mixer_body, t=t, seq=seq, n_exp=n_exp, conv_k=conv_k)
    return pl.pallas_call(
        body,
        grid=(n // t,),
        in_specs=[
            pl.BlockSpec((t, d), cur),
            pl.BlockSpec((t, hw), cur), pl.BlockSpec((HALO, hw), prev), pl.BlockSpec((HALO, hw), nxt),
            pl.BlockSpec((t, hw), cur), pl.BlockSpec((HALO, hw), prev), pl.BlockSpec((HALO, hw), nxt),
            pl.BlockSpec((t, 2 * d), cur),
            _const_spec(wpool.shape), _const_spec(pscale.shape), _const_spec(dww.shape),
            _const_spec(dwb.shape), _const_spec(cng.shape), _const_spec(cnb.shape),
            _const_spec(wco.shape), _const_spec(wout.shape), _const_spec(n2g.shape),
            _const_spec(wr_pad.shape),
        ],
        out_specs=[
            pl.BlockSpec((t, d), cur),
            pl.BlockSpec((t, LANES), cur),
            pl.BlockSpec((n_exp, t), lambda j: (0, j)),
        ],
        out_shape=[
            jax.ShapeDtypeStruct((n, d), f32),
            jax.ShapeDtypeStruct((n, LANES), f32),
            jax.ShapeDtypeStruct((n_exp, n), f32),
        ],
        scratch_shapes=[pltpu.VMEM((t + 2 * HALO, hw), f32), pltpu.VMEM((t + 2 * HALO, hw), f32)],
        compiler_params=_cparams(("arbitrary",)),
        name="mixer",
    )(x2, up, up, up, v, v, v, gates, wpool, pscale, dww, dwb, cng, cnb, wco, wout, n2g, wr_pad)


def _route_body(afft_ref, thr_ref, need_ref, base_ref, *, cap, t, nt):
    bits = pltpu.bitcast(afft_ref[...], i32)
    n_exp = bits.shape[0]
    ntp = base_ref.shape[1]
    cur = jnp.zeros((n_exp, 1), i32)
    for b in range(30, -1, -1):
        cand = cur | (1 << b)
        cnt = jnp.sum((bits >= cand).astype(f32), axis=1, keepdims=True)
        cur = jnp.where(cnt >= cap, cand, cur)
    gt = (bits > cur).astype(f32)
    eq = (bits == cur).astype(f32)
    need = cap - jnp.sum(gt, axis=1, keepdims=True)

    lane = lax.broadcasted_iota(i32, (n_exp, ntp), 1)
    cgt = jnp.zeros((n_exp, ntp), f32)
    ceq = jnp.zeros((n_exp, ntp), f32)
    for jt in range(nt):
        cols = slice(jt * t, (jt + 1) * t)
        cgt = jnp.where(lane == jt, jnp.sum(gt[:, cols], axis=1, keepdims=True), cgt)
        ceq = jnp.where(lane == jt, jnp.sum(eq[:, cols], axis=1, keepdims=True), ceq)
    r = lax.broadcasted_iota(i32, (ntp, ntp), 0)
    c = lax.broadcasted_iota(i32, (ntp, ntp), 1)
    upper = (r < c).astype(f32)
    excl = functools.partial(jnp.dot, preferred_element_type=f32, precision=lax.Precision.HIGHEST)
    eq_before = excl(ceq, upper)
    csel = cgt + jnp.clip(need - eq_before, 0.0, ceq)
    base_ref[...] = excl(csel, upper).astype(i32)
    thr_ref[...] = jnp.broadcast_to(cur, thr_ref.shape)
    need_ref[...] = jnp.broadcast_to(need.astype(i32), need_ref.shape)


def _route(afft, *, cap, t):
    n_exp, n = afft.shape
    nt = n // t
    ntp = -(-(nt + 1) // LANES) * LANES
    body = functools.partial(_route_body, cap=cap, t=t, nt=nt)
    return pl.pallas_call(
        body,
        out_shape=[
            jax.ShapeDtypeStruct((n_exp, LANES), i32),
            jax.ShapeDtypeStruct((n_exp, LANES), i32),
            jax.ShapeDtypeStruct((n_exp, ntp), i32),
        ],
        compiler_params=pltpu.CompilerParams(vmem_limit_bytes=VMEM_LIMIT),
        name="route",
    )(afft)


def _align_down(x, m):
    return (x // m) * m


def _aligned(start):
    return start if isinstance(start, int) else pl.multiple_of(start, BF16_ROWS)


def _dispatch_copy(stage_ref, xe_hbm, sem, e, start):
    return pltpu.make_async_copy(
        stage_ref, xe_hbm.at[e, pl.ds(_aligned(start), SLOT_BLOCK), :], sem)


def _dispatch_body(base_smem, x_ref, n2g_ref, afft_ref, thr_ref, need_ref, xe_hbm,
                   cnt_sel, cnt_eq, carry, stage, stage_x, sems, sem_x, *, t, nt, n_exp, cap):
    j = pl.program_id(0)
    rb = SLOT_BLOCK
    grp = BF16_ROWS

    @pl.when(j == 0)
    def _():
        cnt_sel[...] = jnp.zeros_like(cnt_sel)
        cnt_eq[...] = jnp.zeros_like(cnt_eq)
        carry[...] = jnp.zeros_like(carry)
        stage_x[...] = jnp.zeros_like(stage_x)
        for e in range(n_exp):
            cp = _dispatch_copy(stage_x, xe_hbm, sem_x.at[0], e, cap)
            cp.start()
            cp.wait()

    h2 = _rms(x_ref[...], n2g_ref[...]).astype(bf16)
    bits = pltpu.bitcast(afft_ref[...], i32)
    thr = thr_ref[:, :1]
    need = need_ref[:, :1]
    gt = bits > thr
    eq = bits == thr
    r = lax.broadcasted_iota(i32, (t, t), 0)
    c = lax.broadcasted_iota(i32, (t, t), 1)
    upper = jnp.where(r < c, 1.0, 0.0).astype(bf16)
    eq_f = jnp.where(eq, 1.0, 0.0)
    eq_rank = jnp.dot(eq_f.astype(bf16), upper, preferred_element_type=f32).astype(i32)
    ce = cnt_eq[:, :1]
    sel = gt | (eq & (ce + eq_rank < need))
    sel_f = jnp.where(sel, 1.0, 0.0)
    pos = jnp.dot(sel_f.astype(bf16), upper, preferred_element_type=f32).astype(i32)
    cs = cnt_sel[:, :1]
    slot = cs + pos
    end = cs + jnp.sum(sel_f, axis=1, keepdims=True).astype(i32)
    cnt_sel[...] = jnp.broadcast_to(end, cnt_sel.shape)
    cnt_eq[...] = jnp.broadcast_to(ce + jnp.sum(eq_f, axis=1, keepdims=True).astype(i32), cnt_eq.shape)
    start = _align_down(cs, grp)
    nstart = _align_down(end, grp)
    rel = jnp.where(sel, slot - start, -1)
    relc = jnp.where(sel, slot - nstart, -1)
    keep = nstart == start

    def onehot(relv, rows, shift):
        rr = lax.broadcasted_iota(i32, (rows, t), 0) + shift
        parts = [jnp.where(rr == relv[e:e + 1, :], 1.0, 0.0).astype(bf16) for e in range(n_exp)]
        return jnp.concatenate(parts, axis=0)

    oh = jnp.concatenate([onehot(rel, rb, 0), onehot(relc, grp, 0)], axis=0)
    res = jnp.dot(oh, h2, preferred_element_type=f32)

    nblk = []
    for e in range(n_exp):
        b0 = base_smem[e, j]
        st = _align_down(b0, grp)
        cnt = base_smem[e, j + 1] - b0
        nblk.append(jnp.where(cnt > 0, (b0 - st + cnt + rb - 1) // rb, 0))

        @pl.when(j > 0)
        def _():
            _dispatch_copy(stage.at[e], xe_hbm, sems.at[e], e, 0).wait()

        old = carry[e].astype(f32)
        stage[e, 0:grp, :] = (res[e * rb:e * rb + grp] + old).astype(bf16)
        stage[e, grp:, :] = res[e * rb + grp:(e + 1) * rb].astype(bf16)
        _dispatch_copy(stage.at[e], xe_hbm, sems.at[e], e, st).start()
        part = res[n_exp * rb + e * grp:n_exp * rb + (e + 1) * grp]
        carry[e] = jnp.where(keep[e:e + 1, :], old + part, part).astype(bf16)

    rounds = functools.reduce(jnp.maximum, nblk)

    def extra_round(b, _):
        res_b = jnp.dot(onehot(rel, rb, b * rb), h2, preferred_element_type=f32)
        for e in range(n_exp):
            @pl.when(b < nblk[e])
            def _():
                st = _align_down(base_smem[e, j], grp) + b * rb
                stage_x[...] = res_b[e * rb:(e + 1) * rb].astype(bf16)
                cp = _dispatch_copy(stage_x, xe_hbm, sem_x.at[0], e, st)
                cp.start()
                cp.wait()
        return 0

    lax.fori_loop(1, rounds, extra_round, 0)

    @pl.when(j == nt - 1)
    def _():
        for e in range(n_exp):
            _dispatch_copy(stage.at[e], xe_hbm, sems.at[e], e, 0).wait()


def _dispatch(base, x_mid, n2g, afft, thr, need, *, t, cap):
    n, d = x_mid.shape
    n_exp = afft.shape[0]
    nt = n // t
    body = functools.partial(_dispatch_body, t=t, nt=nt, n_exp=n_exp, cap=cap)
    grid_spec = pltpu.PrefetchScalarGridSpec(
        num_scalar_prefetch=1,
        grid=(nt,),
        in_specs=[
            pl.BlockSpec((t, d), lambda j, b: (j, 0)),
            pl.BlockSpec((1, d), lambda j, b: (0, 0)),
            pl.BlockSpec((n_exp, t), lambda j, b: (0, j)),
            pl.BlockSpec((n_exp, LANES), lambda j, b: (0, 0)),
            pl.BlockSpec((n_exp, LANES), lambda j, b: (0, 0)),
        ],
        out_specs=pl.BlockSpec(memory_space=pl.ANY),
        scratch_shapes=[
            pltpu.VMEM((n_exp, LANES), i32),
            pltpu.VMEM((n_exp, LANES), i32),
            pltpu.VMEM((n_exp, BF16_ROWS, d), bf16),
            pltpu.VMEM((n_exp, SLOT_BLOCK, d), bf16),
            pltpu.VMEM((SLOT_BLOCK, d), bf16),
            pltpu.SemaphoreType.DMA((n_exp,)),
            pltpu.SemaphoreType.DMA((1,)),
        ],
    )
    return pl.pallas_call(
        body,
        grid_spec=grid_spec,
        out_shape=jax.ShapeDtypeStruct((n_exp, cap + SLOT_BLOCK, d), bf16),
        compiler_params=_cparams(("arbitrary",)),
        name="dispatch",
    )(base, x_mid, n2g, afft, thr, need)


def _ffn_up_body(xe_ref, wg_ref, wu_ref, hid_ref, wg_s, wu_s):
    @pl.when(pl.program_id(2) == 0)
    def _():
        wg_s[...] = wg_ref[...].astype(bf16)
        wu_s[...] = wu_ref[...].astype(bf16)

    x = xe_ref[...]
    g = jnp.dot(x, wg_s[...], preferred_element_type=f32)
    u = jnp.dot(x, wu_s[...], preferred_element_type=f32)
    hid_ref[...] = (g * _sigmoid(g) * u).astype(bf16)


def _ffn_up(xe, w_gate, w_up, layer, *, cap, tm, tn):
    n_exp, _, d = xe.shape
    ff = w_gate.shape[-1]
    wspec = pl.BlockSpec((None, None, d, tn), lambda e, jf, i: (layer, e, 0, jf))
    return pl.pallas_call(
        _ffn_up_body,
        grid=(n_exp, ff // tn, cap // tm),
        in_specs=[pl.BlockSpec((None, tm, d), lambda e, jf, i: (e, i, 0)), wspec, wspec],
        out_specs=pl.BlockSpec((None, tm, tn), lambda e, jf, i: (e, i, jf)),
        out_shape=jax.ShapeDtypeStruct((n_exp, cap, ff), bf16),
        scratch_shapes=[pltpu.VMEM((d, tn), bf16), pltpu.VMEM((d, tn), bf16)],
        compiler_params=_cparams(("arbitrary", "arbitrary", "arbitrary")),
        name="ffn_up",
    )(xe, w_gate, w_up)


def _ffn_down_body(hid_ref, wd_ref, ye_ref, wd_s):
    @pl.when(pl.program_id(2) == 0)
    def _():
        wd_s[...] = wd_ref[...].astype(bf16)

    ye_ref[...] = jnp.dot(hid_ref[...], wd_s[...], preferred_element_type=f32).astype(bf16)


def _ffn_down(hid, w_down, layer, *, tm, tn):
    n_exp, cap, ff = hid.shape
    d = w_down.shape[-1]
    return pl.pallas_call(
        _ffn_down_body,
        grid=(n_exp, d // tn, cap // tm),
        in_specs=[
            pl.BlockSpec((None, tm, ff), lambda e, jd, i: (e, i, 0)),
            pl.BlockSpec((None, None, ff, tn), lambda e, jd, i: (layer, e, 0, jd)),
        ],
        out_specs=pl.BlockSpec((None, tm, tn), lambda e, jd, i: (e, i, jd)),
        out_shape=jax.ShapeDtypeStruct((n_exp, cap, d), bf16),
        scratch_shapes=[pltpu.VMEM((ff, tn), bf16)],
        compiler_params=_cparams(("arbitrary", "arbitrary", "arbitrary")),
        name="ffn_down",
    )(hid, w_down)


def _combine_copy(ye_hbm, win, sem, e, start):
    return pltpu.make_async_copy(
        ye_hbm.at[e, pl.ds(_aligned(start), SLOT_BLOCK), :],
        win.at[pl.ds(e * SLOT_BLOCK, SLOT_BLOCK), :], sem)


def _combine_body(base_smem, x_ref, aff_ref, thr_ref, need_ref, fg_ref, ye_hbm, out_ref,
                  cnt_sel, cnt_eq, win, sems, *, t, cap, n_exp, final):
    j = pl.program_id(0)
    rb = SLOT_BLOCK
    grp = BF16_ROWS
    last = cap - rb

    @pl.when(j == 0)
    def _():
        cnt_sel[...] = jnp.zeros_like(cnt_sel)
        cnt_eq[...] = jnp.zeros_like(cnt_eq)

    nblk = []
    for e in range(n_exp):
        b0 = base_smem[e, j]
        st = _align_down(b0, grp)
        cnt = base_smem[e, j + 1] - b0
        nblk.append(jnp.where(cnt > 0, (b0 - st + cnt + rb - 1) // rb, 0))
        _combine_copy(ye_hbm, win, sems.at[e], e, jnp.minimum(st, last)).start()
    rounds = functools.reduce(jnp.maximum, nblk)

    aff = aff_ref[...]
    bits = pltpu.bitcast(aff, i32)
    thr = thr_ref[:1, :]
    need = need_ref[:1, :]
    gt = bits > thr
    eq = bits == thr
    r = lax.broadcasted_iota(i32, (t, t), 0)
    c = lax.broadcasted_iota(i32, (t, t), 1)
    lower = jnp.where(c < r, 1.0, 0.0).astype(bf16)
    eq_f = jnp.where(eq, 1.0, 0.0)
    eq_rank = jnp.dot(lower, eq_f.astype(bf16), preferred_element_type=f32).astype(i32)
    ce = cnt_eq[:1, :]
    sel = gt | (eq & (ce + eq_rank < need))
    sel_f = jnp.where(sel, 1.0, 0.0)
    pos = jnp.dot(lower, sel_f.astype(bf16), preferred_element_type=f32).astype(i32)
    cs = cnt_sel[:1, :]
    slot = cs + pos
    cnt_sel[...] = jnp.broadcast_to(cs + jnp.sum(sel_f, axis=0, keepdims=True).astype(i32), cnt_sel.shape)
    cnt_eq[...] = jnp.broadcast_to(ce + jnp.sum(eq_f, axis=0, keepdims=True).astype(i32), cnt_eq.shape)
    start = _align_down(cs, grp)
    g_hi = aff.astype(bf16).astype(f32)
    g_lo = (aff - g_hi).astype(bf16).astype(f32)

    def weighted_onehots(b):
        first = start + b * rb
        rel = jnp.where(sel & (slot >= first), slot - jnp.minimum(first, last), -1)
        rr = lax.broadcasted_iota(i32, (t, rb), 1)
        hi, lo = [], []
        for e in range(n_exp):
            hit = rr == rel[:, e:e + 1]
            hi.append(jnp.where(hit, g_hi[:, e:e + 1], 0.0).astype(bf16))
            lo.append(jnp.where(hit, g_lo[:, e:e + 1], 0.0).astype(bf16))
        return jnp.concatenate(hi, axis=1), jnp.concatenate(lo, axis=1)

    oh_hi, oh_lo = weighted_onehots(0)
    for e in range(n_exp):
        _combine_copy(ye_hbm, win, sems.at[e], e, 0).wait()
    w = win[...]
    acc = x_ref[...] + jnp.dot(oh_hi, w, preferred_element_type=f32)
    acc = acc + jnp.dot(oh_lo, w, preferred_element_type=f32)

    def extra_round(b, acc):
        for e in range(n_exp):
            @pl.when(b < nblk[e])
            def _():
                st = _align_down(base_smem[e, j], grp) + b * rb
                cp = _combine_copy(ye_hbm, win, sems.at[e], e, jnp.minimum(st, last))
                cp.start()
                cp.wait()
        hi_b, lo_b = weighted_onehots(b)
        wb = win[...]
        acc = acc + jnp.dot(hi_b, wb, preferred_element_type=f32)
        return acc + jnp.dot(lo_b, wb, preferred_element_type=f32)

    acc = lax.fori_loop(1, rounds, extra_round, acc)
    if final:
        acc = _rms(acc, fg_ref[...])
    out_ref[...] = acc


def _combine(base, x_mid, aff, thr_row, need_row, final_g, ye, *, t, cap, final):
    n, d = x_mid.shape
    n_exp = ye.shape[0]
    body = functools.partial(_combine_body, t=t, cap=cap, n_exp=n_exp, final=final)
    grid_spec = pltpu.PrefetchScalarGridSpec(
        num_scalar_prefetch=1,
        grid=(n // t,),
        in_specs=[
            pl.BlockSpec((t, d), lambda j, b: (j, 0)),
            pl.BlockSpec((t, LANES), lambda j, b: (j, 0)),
            pl.BlockSpec((8, LANES), lambda j, b: (0, 0)),
            pl.BlockSpec((8, LANES), lambda j, b: (0, 0)),
            pl.BlockSpec((1, d), lambda j, b: (0, 0)),
            pl.BlockSpec(memory_space=pl.ANY),
        ],
        out_specs=pl.BlockSpec((t, d), lambda j, b: (j, 0)),
        scratch_shapes=[
            pltpu.VMEM((8, LANES), i32),
            pltpu.VMEM((8, LANES), i32),
            pltpu.VMEM((n_exp * SLOT_BLOCK, d), bf16),
            pltpu.SemaphoreType.DMA((n_exp,)),
        ],
    )
    return pl.pallas_call(
        body,
        grid_spec=grid_spec,
        out_shape=jax.ShapeDtypeStruct((n, d), f32),
        compiler_params=_cparams(("arbitrary",)),
        name="combine",
    )(base, x_mid, aff, thr_row, need_row, final_g, ye)


def _pick(n, options):
    for o in options:
        if n % o == 0:
            return o
    raise ValueError(f"no tile in {options} divides {n}")


def _prep_layer_weights(w_in, w_pool, w_conv_out, w_out, w_router):
    d = w_in.shape[1]
    tn = d // 4
    q = tn // 2
    hw = d // 2
    a = w_in[:, :, hw:d]
    g = w_in[:, :, d:d + hw]
    conv_cols = []
    for c0 in range(0, hw, q):
        conv_cols += [a[:, :, c0:c0 + q], g[:, :, c0:c0 + q]]
    w_perm = jnp.concatenate([w_in[:, :, :hw]] + conv_cols + [w_in[:, :, d + hw:]], axis=2)
    n_exp = w_router.shape[-1]
    wr_pad = jnp.pad(w_router, ((0, 0), (0, 0), (0, LANES - n_exp)))
    return (w_perm.astype(bf16), w_pool.astype(bf16), w_conv_out.astype(bf16),
            w_out.astype(bf16), wr_pad, tn)


def _run_trunk(x, params, prepped):
    (norm1_g, pool_scale, dw_w, dw_b, cn_g, cn_b, norm2_g, w_gate, w_up, w_down, final_g) = params
    w_perm, w_pool, w_conv_out, w_out, wr_pad, tn_in = prepped
    b, seq, d = x.shape
    n = b * seq
    depth = norm1_g.shape[0]
    n_exp = w_gate.shape[1]
    ff = w_gate.shape[-1]
    cap = max(1, CAPACITY_FACTOR * n // n_exp)
    t = TOKEN_TILE
    assert seq % t == 0 and cap % SLOT_BLOCK == 0 and cap >= SLOT_BLOCK and n_exp <= LANES
    tm_in = _pick(n, (1024, 512, 256))
    tm_ffn = _pick(cap, (1024, 512, 256, 128))
    tn_ffn = _pick(ff, (512, 256, 128))
    tn_down = _pick(d, (512, 256, 128))
    row = lambda a: a.reshape(1, -1)

    x2 = x.reshape(n, d)
    for l in range(depth):
        up, v, gates = _in_proj(x2, row(norm1_g[l]), w_perm[l], tm=tm_in, tn=tn_in)
        x_mid, aff, afft = _mixer(
            x2, up, v, gates, w_pool[l], row(pool_scale[l]), dw_w[l], row(dw_b[l]), row(cn_g[l]),
            row(cn_b[l]), w_conv_out[l], w_out[l], row(norm2_g[l]), wr_pad[l],
            t=t, seq=seq, n_exp=n_exp)
        thr, need, base = _route(afft, cap=cap, t=t)
        thr_row = jnp.full((8, LANES), INT_MAX, i32).at[:, :n_exp].set(thr[:, 0][None, :])
        need_row = jnp.zeros((8, LANES), i32).at[:, :n_exp].set(need[:, 0][None, :])
        xe = _dispatch(base, x_mid, row(norm2_g[l]), afft, thr, need, t=t, cap=cap)
        hid = _ffn_up(xe, w_gate, w_up, l, cap=cap, tm=tm_ffn, tn=tn_ffn)
        ye = _ffn_down(hid, w_down, l, tm=tm_ffn, tn=tn_down)
        x2 = _combine(base, x_mid, aff, thr_row, need_row, row(final_g), ye,
                      t=t, cap=cap, final=(l == depth - 1))
    return x2.reshape(b, seq, d)


def kernel(x_prompt, x_sample, norm1_g, w_in, w_pool, pool_scale, dw_w, dw_b, cn_g, cn_b,
           w_conv_out, w_out, norm2_g, w_router, w_gate, w_up, w_down, final_g):
    prepped = _prep_layer_weights(w_in, w_pool, w_conv_out, w_out, w_router)
    params = (norm1_g, pool_scale, dw_w, dw_b, cn_g, cn_b, norm2_g, w_gate, w_up, w_down, final_g)
    return (_run_trunk(x_prompt, params, prepped), _run_trunk(x_sample, params, prepped))
```

```python
import functools

import jax
import jax.numpy as jnp
from jax import lax
from jax.experimental import pallas as pl
from jax.experimental.pallas import tpu as pltpu

EPS = 1e-6
POOL_WINDOWS = (2, 4, 8, 16)
CAPACITY_FACTOR = 2

LANES = 128
F32_ROWS = 8
BF16_ROWS = 16
HALO = 16
SLOT_BLOCK = 64
GATE_COLS = 3 * LANES
TOKEN_TILE = 256
VMEM_LIMIT = 58 * 1024 * 1024
INT_MAX = 2**31 - 1

f32 = jnp.float32
bf16 = jnp.bfloat16
i32 = jnp.int32


def _rms(x, g):
    return x * lax.rsqrt(jnp.mean(x * x, axis=-1, keepdims=True) + EPS) * g


def _sigmoid(x):
    return 1.0 / (1.0 + jnp.exp(-x))


def _cparams(sem, limit=VMEM_LIMIT):
    return pltpu.CompilerParams(dimension_semantics=sem, vmem_limit_bytes=limit)


def _in_proj_body(x_ref, g_ref, w_ref, up_ref, v_ref, gate_ref, h_scr, *, n_pool, n_conv):
    j = pl.program_id(1)

    @pl.when(j == 0)
    def _():
        h_scr[...] = _rms(x_ref[...], g_ref[...]).astype(bf16)

    u = jnp.dot(h_scr[...], w_ref[...], preferred_element_type=f32)
    q = u.shape[1] // 2
    sig = _sigmoid(u)
    glu = u[:, :q] * sig[:, q:]
    gate = sig.astype(bf16)

    @pl.when(j < n_pool)
    def _():
        up_ref[...] = u

    @pl.when((j >= n_pool) & (j < n_pool + n_conv))
    def _():
        v_ref[...] = glu

    @pl.when(j >= n_pool + n_conv)
    def _():
        gate_ref[...] = gate


def _in_proj(x2, g1, w_perm, *, tm, tn):
    n, d = x2.shape
    n_pool, n_conv, n_gate = (d // 2) // tn, d // tn, (2 * d) // tn
    q = tn // 2
    body = functools.partial(_in_proj_body, n_pool=n_pool, n_conv=n_conv)
    return pl.pallas_call(
        body,
        grid=(n // tm, n_pool + n_conv + n_gate),
        in_specs=[
            pl.BlockSpec((tm, d), lambda i, j: (i, 0)),
            pl.BlockSpec((1, d), lambda i, j: (0, 0)),
            pl.BlockSpec((d, tn), lambda i, j: (0, j)),
        ],
        out_specs=[
            pl.BlockSpec((tm, tn), lambda i, j: (i, jnp.clip(j, 0, n_pool - 1))),
            pl.BlockSpec((tm, q), lambda i, j: (i, jnp.clip(j - n_pool, 0, n_conv - 1))),
            pl.BlockSpec((tm, tn), lambda i, j: (i, jnp.clip(j - n_pool - n_conv, 0, n_gate - 1))),
        ],
        out_shape=[
            jax.ShapeDtypeStruct((n, d // 2), f32),
            jax.ShapeDtypeStruct((n, d // 2), f32),
            jax.ShapeDtypeStruct((n, 2 * d), bf16),
        ],
        scratch_shapes=[pltpu.VMEM((tm, d), bf16)],
        compiler_params=_cparams(("arbitrary", "arbitrary")),
        name="in_proj",
    )(x2, g1, w_perm)


def _mixer_body(x_ref, upc_ref, upp_ref, upn_ref, vc_ref, vp_ref, vn_ref, gate_ref,
                wpool_ref, pscale_ref, dww_ref, dwb_ref, cng_ref, cnb_ref, wco_ref, wout_ref,
                n2g_ref, wr_ref, xmid_ref, h2_ref, aff_ref, afft_ref, upext, vext, vshift,
                *, t, seq, n_exp, conv_k):
    j = pl.program_id(0)
    s0 = (j * t) % seq
    has_prev = s0 > 0
    has_next = s0 + t < seq
    d = x_ref.shape[1]

    upext[0:HALO, :] = jnp.where(has_prev, upp_ref[...], 0.0)
    upext[HALO:HALO + t, :] = upc_ref[...]
    upext[HALO + t:, :] = jnp.where(has_next, upn_ref[...], 0.0)
    vext[0:HALO, :] = jnp.where(has_prev, vp_ref[...], 0.0)
    vext[HALO:HALO + t, :] = vc_ref[...]
    vext[HALO + t:, :] = jnp.where(has_next, vn_ref[...], 0.0)

    spos = s0 + lax.broadcasted_iota(i32, (t, 1), 0)
    gch = upext.shape[1] // len(POOL_WINDOWS)
    y_pool = []
    for gi, w in enumerate(POOL_WINDOWS):
        left = w // 2
        right = w - 1 - left
        cols = slice(gi * gch, (gi + 1) * gch)
        total = upext[HALO - left:HALO - left + t, cols]
        for dd in range(-left + 1, right + 1):
            total = total + upext[HALO + dd:HALO + dd + t, cols]
        lo = jnp.maximum(spos - left, 0)
        hi = jnp.minimum(spos + right, seq - 1) + 1
        count = (hi - lo).astype(f32)
        mixed = total / count - upext[HALO:HALO + t, cols]
        y_pool.append(jnp.dot(mixed.astype(bf16), wpool_ref[gi], preferred_element_type=f32))
    y_pool = jnp.concatenate(y_pool, axis=1) * pscale_ref[...]

    pad = conv_k // 2
    cw = vext.shape[1]
    sub = vshift.shape[0]
    half = t // 2
    first = HALO - pad
    conv = []
    for c0 in range(0, cw, LANES):
        cols = slice(c0, c0 + LANES)
        for s in range(sub):
            vshift[s] = vext[s:s + vshift.shape[1], cols]
        halves = []
        for r0 in (0, half):
            acc = jnp.zeros((half, LANES), f32) + dwb_ref[:, cols]
            for k in range(conv_k):
                s, a = (first + k) % sub, (first + k) // sub * sub
                acc = acc + dww_ref[k:k + 1, cols] * vshift[s, r0 + a:r0 + a + half, :]
            halves.append(acc)
        conv.append(jnp.concatenate(halves, axis=0))
    conv = jnp.concatenate(conv, axis=1)
    mu = jnp.mean(conv, axis=-1, keepdims=True)
    xc = conv - mu
    var = jnp.mean(xc * xc, axis=-1, keepdims=True)
    ln = xc * lax.rsqrt(var + EPS) * cng_ref[...] + cnb_ref[...]
    act = ln * _sigmoid(ln)
    y_conv = jnp.dot(act.astype(bf16), wco_ref[...], preferred_element_type=f32)

    merged = (gate_ref[:, :d].astype(f32) * y_pool + gate_ref[:, d:].astype(f32) * y_conv)
    x_mid = x_ref[...] + jnp.dot(merged.astype(bf16), wout_ref[...], preferred_element_type=f32)
    xmid_ref[...] = x_mid

    h2 = _rms(x_mid, n2g_ref[...])
    h_hi = h2.astype(bf16)
    h2_ref[...] = h_hi
    h_lo = (h2 - h_hi.astype(f32)).astype(bf16)
    both = jnp.dot(h_hi, wr_ref[...], preferred_element_type=f32)
    logits = (both[:, :LANES] + both[:, LANES:]
              + jnp.dot(h_lo, wr_ref[:, :LANES], preferred_element_type=f32))
    lane = lax.broadcasted_iota(i32, logits.shape, 1)
    logits = jnp.where(lane < n_exp, logits, -1e30)
    ex = jnp.exp(logits - jnp.max(logits, axis=-1, keepdims=True))
    aff = ex / jnp.sum(ex, axis=-1, keepdims=True)
    aff_ref[...] = aff
    afft_ref[...] = aff.T[:n_exp, :]


def _const_spec(shape):
    zeros = (0,) * len(shape)
    return pl.BlockSpec(shape, lambda j: zeros, pipeline_mode=pl.Buffered(1))


def _mixer(x2, up, v, gates, wpool, pscale, dww, dwb, cng, cnb, wco, wout, n2g, wr_pad,
           *, t, seq, n_exp):
    n, d = x2.shape
    hw = d // 2
    nh = n // HALO
    th = t // HALO
    conv_k = dww.shape[0]
    cur = lambda j: (j, 0)
    prev = lambda j: (jnp.maximum(j * th - 1, 0), 0)
    nxt = lambda j: (jnp.minimum((j + 1) * th, nh - 1), 0)
    body = functools.partial(_mixer_body, t=t, seq=seq, n_exp=n_exp, conv_k=conv_k)
    return pl.pallas_call(
        body,
        grid=(n // t,),
        in_specs=[
            pl.BlockSpec((t, d), cur),
            pl.BlockSpec((t, hw), cur), pl.BlockSpec((HALO, hw), prev), pl.BlockSpec((HALO, hw), nxt),
            pl.BlockSpec((t, hw), cur), pl.BlockSpec((HALO, hw), prev), pl.BlockSpec((HALO, hw), nxt),
            pl.BlockSpec((t, 2 * d), cur),
            _const_spec(wpool.shape), _const_spec(pscale.shape), _const_spec(dww.shape),
            _const_spec(dwb.shape), _const_spec(cng.shape), _const_spec(cnb.shape),
            _const_spec(wco.shape), _const_spec(wout.shape), _const_spec(n2g.shape),
            _const_spec(wr_pad.shape),
        ],
        out_specs=[
            pl.BlockSpec((t, d), cur),
            pl.BlockSpec((t, d), cur),
            pl.BlockSpec((t, LANES), cur),
            pl.BlockSpec((n_exp, t), lambda j: (0, j)),
        ],
        out_shape=[
            jax.ShapeDtypeStruct((n, d), f32),
            jax.ShapeDtypeStruct((n, d), bf16),
            jax.ShapeDtypeStruct((n, LANES), f32),
            jax.ShapeDtypeStruct((n_exp, n), f32),
        ],
        scratch_shapes=[pltpu.VMEM((t + 2 * HALO, hw), f32), pltpu.VMEM((t + 2 * HALO, hw), f32),
                        pltpu.VMEM((F32_ROWS, t + 2 * HALO - F32_ROWS, LANES), f32)],
        compiler_params=_cparams(("arbitrary",)),
        name="mixer",
    )(x2, up, up, up, v, v, v, gates, wpool, pscale, dww, dwb, cng, cnb, wco, wout, n2g, wr_pad)


def _route_body(afft_ref, thr_ref, need_ref, base_ref, *, cap, t, nt):
    bits = pltpu.bitcast(afft_ref[...], i32)
    n_exp = bits.shape[0]
    ntp = base_ref.shape[1]
    cur = jnp.zeros((n_exp, 1), i32)
    for b in range(30, -1, -1):
        cand = cur | (1 << b)
        cnt = jnp.sum((bits >= cand).astype(f32), axis=1, keepdims=True)
        cur = jnp.where(cnt >= cap, cand, cur)
    gt = (bits > cur).astype(f32)
    eq = (bits == cur).astype(f32)
    need = cap - jnp.sum(gt, axis=1, keepdims=True)

    lane = lax.broadcasted_iota(i32, (n_exp, ntp), 1)
    cgt = jnp.zeros((n_exp, ntp), f32)
    ceq = jnp.zeros((n_exp, ntp), f32)
    for jt in range(nt):
        cols = slice(jt * t, (jt + 1) * t)
        cgt = jnp.where(lane == jt, jnp.sum(gt[:, cols], axis=1, keepdims=True), cgt)
        ceq = jnp.where(lane == jt, jnp.sum(eq[:, cols], axis=1, keepdims=True), ceq)
    r = lax.broadcasted_iota(i32, (ntp, ntp), 0)
    c = lax.broadcasted_iota(i32, (ntp, ntp), 1)
    upper = (r < c).astype(f32)
    excl = functools.partial(jnp.dot, preferred_element_type=f32, precision=lax.Precision.HIGHEST)
    eq_before = excl(ceq, upper)
    csel = cgt + jnp.clip(need - eq_before, 0.0, ceq)
    base_ref[...] = excl(csel, upper).astype(i32)
    thr_ref[...] = jnp.broadcast_to(cur, thr_ref.shape)
    need_ref[...] = jnp.broadcast_to(need.astype(i32), need_ref.shape)


def _route(afft, *, cap, t):
    n_exp, n = afft.shape
    nt = n // t
    ntp = -(-(nt + 1) // LANES) * LANES
    body = functools.partial(_route_body, cap=cap, t=t, nt=nt)
    return pl.pallas_call(
        body,
        out_shape=[
            jax.ShapeDtypeStruct((n_exp, LANES), i32),
            jax.ShapeDtypeStruct((n_exp, LANES), i32),
            jax.ShapeDtypeStruct((n_exp, ntp), i32),
        ],
        compiler_params=pltpu.CompilerParams(vmem_limit_bytes=VMEM_LIMIT),
        name="route",
    )(afft)


def _align_down(x, m):
    return (x // m) * m


def _aligned(start):
    return start if isinstance(start, int) else pl.multiple_of(start, BF16_ROWS)


def _dispatch_copy(stage_ref, xe_hbm, sem, e, start):
    return pltpu.make_async_copy(
        stage_ref, xe_hbm.at[e, pl.ds(_aligned(start), SLOT_BLOCK), :], sem)


def _dispatch_body(base_smem, h2_ref, aff_ref, afft_ref, thr_ref, need_ref, tri_ref, exw_ref, exc_ref, xe_hbm,
                   cnt_sel, cnt_eq, carry, stage, stage_x, sems, sem_x, *, t, nt, n_exp, cap):
    j = pl.program_id(0)
    rb = SLOT_BLOCK
    grp = BF16_ROWS

    @pl.when(j == 0)
    def _():
        cnt_sel[...] = jnp.zeros_like(cnt_sel)
        cnt_eq[...] = jnp.zeros_like(cnt_eq)
        carry[...] = jnp.zeros_like(carry)
        stage_x[...] = jnp.zeros_like(stage_x)
        for e in range(n_exp):
            cp = _dispatch_copy(stage_x, xe_hbm, sem_x.at[0], e, cap)
            cp.start()
            cp.wait()

    aff = aff_ref[...]
    g_hi = aff.astype(bf16)
    rest = aff - g_hi.astype(f32)
    g_mid = rest.astype(bf16)
    g_lo = (rest - g_mid.astype(f32)).astype(bf16)
    h2 = jnp.concatenate([h2_ref[...], g_hi, g_mid, g_lo], axis=1)
    bits = pltpu.bitcast(afft_ref[...], i32)
    thr = thr_ref[:, :1]
    need = need_ref[:, :1]
    gt = bits > thr
    eq = bits == thr
    upper = tri_ref[...]
    eq_f = jnp.where(eq, 1.0, 0.0)
    eq_rank = jnp.dot(eq_f.astype(bf16), upper, preferred_element_type=f32).astype(i32)
    ce = cnt_eq[:, :1]
    sel = gt | (eq & (ce + eq_rank < need))
    sel_f = jnp.where(sel, 1.0, 0.0)
    pos = jnp.dot(sel_f.astype(bf16), upper, preferred_element_type=f32).astype(i32)
    cs = cnt_sel[:, :1]
    slot = cs + pos
    end = cs + jnp.sum(sel_f, axis=1, keepdims=True).astype(i32)
    cnt_sel[...] = jnp.broadcast_to(end, cnt_sel.shape)
    cnt_eq[...] = jnp.broadcast_to(ce + jnp.sum(eq_f, axis=1, keepdims=True).astype(i32), cnt_eq.shape)
    start = _align_down(cs, grp)
    nstart = _align_down(end, grp)
    rel = slot - start
    relc = slot - nstart
    fill = jnp.full((LANES - n_exp, t), -1.0, f32)

    def onehot(relv, rows, ex_ref):
        relv = jnp.where(sel & (relv >= 0) & (relv < rows), relv, -1).astype(f32)
        relv = jnp.concatenate([relv, fill], axis=0).astype(bf16)
        spread = jnp.dot(ex_ref[...], relv, preferred_element_type=f32)
        rr = lax.broadcasted_iota(i32, spread.shape, 0) % rows
        return jnp.where(spread.astype(i32) == rr, 1.0, 0.0).astype(bf16)

    oh = jnp.concatenate([onehot(rel, rb, exw_ref), onehot(relc, grp, exc_ref)], axis=0)
    res = jnp.dot(oh, h2, preferred_element_type=f32)

    nblk = []
    for e in range(n_exp):
        b0 = base_smem[e, j]
        st = _align_down(b0, grp)
        b1 = base_smem[e, j + 1]
        cnt = b1 - b0
        nblk.append(jnp.where(cnt > 0, (b0 - st + cnt + rb - 1) // rb, 0))

        @pl.when(j > 0)
        def _():
            _dispatch_copy(stage.at[e], xe_hbm, sems.at[e], e, 0).wait()

        old = carry[e]
        stage[e, 0:grp, :] = (res[e * rb:e * rb + grp] + old).astype(bf16)
        stage[e, grp:, :] = res[e * rb + grp:(e + 1) * rb].astype(bf16)
        _dispatch_copy(stage.at[e], xe_hbm, sems.at[e], e, st).start()
        part = res[n_exp * rb + e * grp:n_exp * rb + (e + 1) * grp]
        carry[e] = jnp.where(_align_down(b1, grp) == st, old + part, part)

    rounds = functools.reduce(jnp.maximum, nblk)

    def extra_round(b, _):
        res_b = jnp.dot(onehot(rel - b * rb, rb, exw_ref), h2, preferred_element_type=f32)
        for e in range(n_exp):
            @pl.when(b < nblk[e])
            def _():
                st = _align_down(base_smem[e, j], grp) + b * rb
                stage_x[...] = res_b[e * rb:(e + 1) * rb].astype(bf16)
                cp = _dispatch_copy(stage_x, xe_hbm, sem_x.at[0], e, st)
                cp.start()
                cp.wait()
        return 0

    lax.fori_loop(1, rounds, extra_round, 0)

    @pl.when(j == nt - 1)
    def _():
        for e in range(n_exp):
            _dispatch_copy(stage.at[e], xe_hbm, sems.at[e], e, 0).wait()


def _dispatch(base, h2, aff, afft, thr, need, tri, exw, exc, *, t, cap):
    n, d = h2.shape
    width = d + GATE_COLS
    n_exp = afft.shape[0]
    nt = n // t
    body = functools.partial(_dispatch_body, t=t, nt=nt, n_exp=n_exp, cap=cap)
    grid_spec = pltpu.PrefetchScalarGridSpec(
        num_scalar_prefetch=1,
        grid=(nt,),
        in_specs=[
            pl.BlockSpec((t, d), lambda j, b: (j, 0)),
            pl.BlockSpec((t, LANES), lambda j, b: (j, 0)),
            pl.BlockSpec((n_exp, t), lambda j, b: (0, j)),
            pl.BlockSpec((n_exp, LANES), lambda j, b: (0, 0)),
            pl.BlockSpec((n_exp, LANES), lambda j, b: (0, 0)),
            pl.BlockSpec((t, t), lambda j, b: (0, 0)),
            pl.BlockSpec(exw.shape, lambda j, b: (0, 0)),
            pl.BlockSpec(exc.shape, lambda j, b: (0, 0)),
        ],
        out_specs=pl.BlockSpec(memory_space=pl.ANY),
        scratch_shapes=[
            pltpu.VMEM((n_exp, LANES), i32),
            pltpu.VMEM((n_exp, LANES), i32),
            pltpu.VMEM((n_exp, BF16_ROWS, width), f32),
            pltpu.VMEM((n_exp, SLOT_BLOCK, width), bf16),
            pltpu.VMEM((SLOT_BLOCK, width), bf16),
            pltpu.SemaphoreType.DMA((n_exp,)),
            pltpu.SemaphoreType.DMA((1,)),
        ],
    )
    return pl.pallas_call(
        body,
        grid_spec=grid_spec,
        out_shape=jax.ShapeDtypeStruct((n_exp, cap + SLOT_BLOCK, width), bf16),
        compiler_params=_cparams(("arbitrary",)),
        name="dispatch",
    )(base, h2, aff, afft, thr, need, tri, exw, exc)


def _ffn_up_body(xe_ref, wg_ref, wu_ref, hid_ref, wg_s, wu_s):
    @pl.when(pl.program_id(2) == 0)
    def _():
        wg_s[...] = wg_ref[...].astype(bf16)
        wu_s[...] = wu_ref[...].astype(bf16)

    x = xe_ref[...]
    g = jnp.dot(x, wg_s[...], preferred_element_type=f32)
    u = jnp.dot(x, wu_s[...], preferred_element_type=f32)
    hid_ref[...] = (g * _sigmoid(g) * u).astype(bf16)


def _ffn_up(xe, w_gate, w_up, layer, *, cap, tm, tn):
    n_exp = xe.shape[0]
    d, ff = w_gate.shape[-2:]
    wspec = pl.BlockSpec((None, None, d, tn), lambda e, jf, i: (layer, e, 0, jf))
    return pl.pallas_call(
        _ffn_up_body,
        grid=(n_exp, ff // tn, cap // tm),
        in_specs=[pl.BlockSpec((None, tm, d), lambda e, jf, i: (e, i, 0)), wspec, wspec],
        out_specs=pl.BlockSpec((None, tm, tn), lambda e, jf, i: (e, i, jf)),
        out_shape=jax.ShapeDtypeStruct((n_exp, cap, ff), bf16),
        scratch_shapes=[pltpu.VMEM((d, tn), bf16), pltpu.VMEM((d, tn), bf16)],
        compiler_params=_cparams(("arbitrary", "arbitrary", "arbitrary")),
        name="ffn_up",
    )(xe, w_gate, w_up)


def _ffn_down_body(hid_ref, ghi_ref, gmid_ref, glo_ref, wd_ref, ye_ref, wd_s):
    @pl.when(pl.program_id(2) == 0)
    def _():
        wd_s[...] = wd_ref[...].astype(bf16)

    pieces = ghi_ref[...].astype(f32) + gmid_ref[...].astype(f32) + glo_ref[...].astype(f32)
    lane = lax.broadcasted_iota(i32, pieces.shape, 1)
    gate = jnp.sum(jnp.where(lane == pl.program_id(0), pieces, 0.0), axis=1, keepdims=True)
    y = jnp.dot(hid_ref[...], wd_s[...], preferred_element_type=f32)
    ye_ref[...] = (y * gate).astype(bf16)


def _ffn_down(hid, xe, w_down, layer, *, tm, tn):
    n_exp, cap, ff = hid.shape
    d = w_down.shape[-1]
    gcol = d // LANES
    gspec = lambda k: pl.BlockSpec((None, tm, LANES), lambda e, jd, i: (e, i, gcol + k))
    return pl.pallas_call(
        _ffn_down_body,
        grid=(n_exp, d // tn, cap // tm),
        in_specs=[
            pl.BlockSpec((None, tm, ff), lambda e, jd, i: (e, i, 0)),
            gspec(0), gspec(1), gspec(2),
            pl.BlockSpec((None, None, ff, tn), lambda e, jd, i: (layer, e, 0, jd)),
        ],
        out_specs=pl.BlockSpec((None, tm, tn), lambda e, jd, i: (e, i, jd)),
        out_shape=jax.ShapeDtypeStruct((n_exp, cap, d), bf16),
        scratch_shapes=[pltpu.VMEM((ff, tn), bf16)],
        compiler_params=_cparams(("arbitrary", "arbitrary", "arbitrary")),
        name="ffn_down",
    )(hid, xe, xe, xe, w_down)


def _combine_copy(ye_hbm, win, sems, buf, e, start):
    return pltpu.make_async_copy(
        ye_hbm.at[e, pl.ds(_aligned(start), SLOT_BLOCK), :],
        win.at[buf, pl.ds(e * SLOT_BLOCK, SLOT_BLOCK), :], sems.at[buf, e])


def _combine_body(base_smem, x_ref, aff_ref, thr_ref, need_ref, fg_ref, tri_ref, exp_ref, ye_hbm, out_ref,
                  cnt_sel, cnt_eq, win, sems, *, t, nt, cap, n_exp, final):
    j = pl.program_id(0)
    rb = SLOT_BLOCK
    grp = BF16_ROWS
    last = cap - rb
    buf = j % 2

    def start_windows(step, into):
        for e in range(n_exp):
            st = _align_down(base_smem[e, step], grp)
            _combine_copy(ye_hbm, win, sems, into, e, jnp.minimum(st, last)).start()

    @pl.when(j == 0)
    def _():
        start_windows(0, 0)

    @pl.when(j + 1 < nt)
    def _():
        start_windows(j + 1, 1 - buf)

    @pl.when(j == 0)
    def _():
        cnt_sel[...] = jnp.zeros_like(cnt_sel)
        cnt_eq[...] = jnp.zeros_like(cnt_eq)

    nblk = []
    for e in range(n_exp):
        b0 = base_smem[e, j]
        st = _align_down(b0, grp)
        cnt = base_smem[e, j + 1] - b0
        nblk.append(jnp.where(cnt > 0, (b0 - st + cnt + rb - 1) // rb, 0))
    rounds = functools.reduce(jnp.maximum, nblk)

    aff = aff_ref[...]
    bits = pltpu.bitcast(aff, i32)
    thr = thr_ref[:1, :]
    need = need_ref[:1, :]
    gt = bits > thr
    eq = bits == thr
    lower = tri_ref[...]
    eq_f = jnp.where(eq, 1.0, 0.0)
    eq_rank = jnp.dot(lower, eq_f.astype(bf16), preferred_element_type=f32).astype(i32)
    ce = cnt_eq[:1, :]
    sel = gt | (eq & (ce + eq_rank < need))
    sel_f = jnp.where(sel, 1.0, 0.0)
    pos = jnp.dot(lower, sel_f.astype(bf16), preferred_element_type=f32).astype(i32)
    cs = cnt_sel[:1, :]
    slot = cs + pos
    cnt_sel[...] = jnp.broadcast_to(cs + jnp.sum(sel_f, axis=0, keepdims=True).astype(i32), cnt_sel.shape)
    cnt_eq[...] = jnp.broadcast_to(ce + jnp.sum(eq_f, axis=0, keepdims=True).astype(i32), cnt_eq.shape)
    start = _align_down(cs, grp)
    lane_row = lax.broadcasted_iota(i32, (t, n_exp * rb), 1) % rb

    def onehots(b):
        first = start + b * rb
        rel = slot - jnp.minimum(first, last)
        rel = jnp.where(sel & (slot >= first) & (rel < rb), rel, -1)
        spread = jnp.dot(rel.astype(f32).astype(bf16), exp_ref[...], preferred_element_type=f32)
        return jnp.where(spread.astype(i32) == lane_row, 1.0, 0.0).astype(bf16)

    oh = onehots(0)
    for e in range(n_exp):
        _combine_copy(ye_hbm, win, sems, buf, e, 0).wait()
    acc = x_ref[...] + jnp.dot(oh, win[buf], preferred_element_type=f32)

    def extra_round(b, acc):
        for e in range(n_exp):
            @pl.when(b < nblk[e])
            def _():
                st = _align_down(base_smem[e, j], grp) + b * rb
                cp = _combine_copy(ye_hbm, win, sems, buf, e, jnp.minimum(st, last))
                cp.start()
                cp.wait()
        return acc + jnp.dot(onehots(b), win[buf], preferred_element_type=f32)

    acc = lax.fori_loop(1, rounds, extra_round, acc)
    if final:
        acc = _rms(acc, fg_ref[...])
    out_ref[...] = acc


def _combine(base, x_mid, aff, thr_row, need_row, final_g, tri, expand, ye, *, t, cap, final):
    n, d = x_mid.shape
    n_exp = ye.shape[0]
    body = functools.partial(_combine_body, t=t, nt=n // t, cap=cap, n_exp=n_exp, final=final)
    grid_spec = pltpu.PrefetchScalarGridSpec(
        num_scalar_prefetch=1,
        grid=(n // t,),
        in_specs=[
            pl.BlockSpec((t, d), lambda j, b: (j, 0)),
            pl.BlockSpec((t, LANES), lambda j, b: (j, 0)),
            pl.BlockSpec((8, LANES), lambda j, b: (0, 0)),
            pl.BlockSpec((8, LANES), lambda j, b: (0, 0)),
            pl.BlockSpec((1, d), lambda j, b: (0, 0)),
            pl.BlockSpec((t, t), lambda j, b: (0, 0)),
            pl.BlockSpec(expand.shape, lambda j, b: (0, 0)),
            pl.BlockSpec(memory_space=pl.ANY),
        ],
        out_specs=pl.BlockSpec((t, d), lambda j, b: (j, 0)),
        scratch_shapes=[
            pltpu.VMEM((8, LANES), i32),
            pltpu.VMEM((8, LANES), i32),
            pltpu.VMEM((2, n_exp * SLOT_BLOCK, d), bf16),
            pltpu.SemaphoreType.DMA((2, n_exp)),
        ],
    )
    return pl.pallas_call(
        body,
        grid_spec=grid_spec,
        out_shape=jax.ShapeDtypeStruct((n, d), f32),
        compiler_params=_cparams(("arbitrary",)),
        name="combine",
    )(base, x_mid, aff, thr_row, need_row, final_g, tri, expand, ye)


def _pick(n, options):
    for o in options:
        if n % o == 0:
            return o
    raise ValueError(f"no tile in {options} divides {n}")


def _prep_layer_weights(w_in, w_pool, w_conv_out, w_out, w_router):
    d = w_in.shape[1]
    tn = d // 4
    q = tn // 2
    hw = d // 2
    a = w_in[:, :, hw:d]
    g = w_in[:, :, d:d + hw]
    conv_cols = []
    for c0 in range(0, hw, q):
        conv_cols += [a[:, :, c0:c0 + q], g[:, :, c0:c0 + q]]
    w_perm = jnp.concatenate([w_in[:, :, :hw]] + conv_cols + [w_in[:, :, d + hw:]], axis=2)
    n_exp = w_router.shape[-1]
    wr_f32 = jnp.pad(w_router, ((0, 0), (0, 0), (0, LANES - n_exp)))
    wr_hi = wr_f32.astype(bf16)
    wr_lo = (wr_f32 - wr_hi.astype(f32)).astype(bf16)
    wr_pad = jnp.concatenate([wr_hi, wr_lo], axis=2)
    return (w_perm.astype(bf16), w_pool.astype(bf16), w_conv_out.astype(bf16),
            w_out.astype(bf16), wr_pad, tn)


def _run_trunk(x, params, prepped):
    (norm1_g, pool_scale, dw_w, dw_b, cn_g, cn_b, norm2_g, w_gate, w_up, w_down, final_g) = params
    w_perm, w_pool, w_conv_out, w_out, wr_pad, tn_in = prepped
    b, seq, d = x.shape
    n = b * seq
    depth = norm1_g.shape[0]
    n_exp = w_gate.shape[1]
    ff = w_gate.shape[-1]
    cap = max(1, CAPACITY_FACTOR * n // n_exp)
    t = TOKEN_TILE
    assert seq % t == 0 and cap % SLOT_BLOCK == 0 and cap >= SLOT_BLOCK and n_exp <= LANES
    tm_in = _pick(n, (1024, 512, 256))
    tm_ffn = _pick(cap, (1024, 512, 256, 128))
    tn_ffn = _pick(ff, (512, 256, 128))
    tn_down = _pick(d, (512, 256, 128))
    row = lambda a: a.reshape(1, -1)
    upper = jnp.triu(jnp.ones((t, t), bf16), 1)
    lower = jnp.tril(jnp.ones((t, t), bf16), -1)
    spread = lambda rows: (jnp.arange(LANES)[:, None] == jnp.arange(n_exp * rows)[None, :] // rows).astype(bf16)
    expand = spread(SLOT_BLOCK)
    exw, exc = expand.T, spread(BF16_ROWS).T

    x2 = x.reshape(n, d)
    for l in range(depth):
        up, v, gates = _in_proj(x2, row(norm1_g[l]), w_perm[l], tm=tm_in, tn=tn_in)
        x_mid, h2, aff, afft = _mixer(
            x2, up, v, gates, w_pool[l], row(pool_scale[l]), dw_w[l], row(dw_b[l]), row(cn_g[l]),
            row(cn_b[l]), w_conv_out[l], w_out[l], row(norm2_g[l]), wr_pad[l],
            t=t, seq=seq, n_exp=n_exp)
        thr, need, base = _route(afft, cap=cap, t=t)
        thr_row = jnp.full((8, LANES), INT_MAX, i32).at[:, :n_exp].set(thr[:, 0][None, :])
        need_row = jnp.zeros((8, LANES), i32).at[:, :n_exp].set(need[:, 0][None, :])
        xe = _dispatch(base, h2, aff, afft, thr, need, upper, exw, exc, t=t, cap=cap)
        hid = _ffn_up(xe, w_gate, w_up, l, cap=cap, tm=tm_ffn, tn=tn_ffn)
        ye = _ffn_down(hid, xe, w_down, l, tm=tm_ffn, tn=tn_down)
        x2 = _combine(base, x_mid, aff, thr_row, need_row, row(final_g), lower, expand, ye,
                      t=t, cap=cap, final=(l == depth - 1))
    return x2.reshape(b, seq, d)


def kernel(x_prompt, x_sample, norm1_g, w_in, w_pool, pool_scale, dw_w, dw_b, cn_g, cn_b,
           w_conv_out, w_out, norm2_g, w_router, w_gate, w_up, w_down, final_g):
    prepped = _prep_layer_weights(w_in, w_pool, w_conv_out, w_out, w_router)
    params = (norm1_g, pool_scale, dw_w, dw_b, cn_g, cn_b, norm2_g, w_gate, w_up, w_down, final_g)
    return (_run_trunk(x_prompt, params, prepped), _run_trunk(x_sample, params, prepped))
```

```python
import functools

import jax
import jax.numpy as jnp
from jax import lax
from jax.experimental import pallas as pl
from jax.experimental.pallas import tpu as pltpu

EPS = 1e-6
POOL_WINDOWS = (2, 4, 8, 16)
CAPACITY_FACTOR = 2

LANES = 128
F32_ROWS = 8
BF16_ROWS = 16
HALO = 16
SLOT_BLOCK = 64
GATE_COLS = 3 * LANES
TOKEN_TILE = 256
VMEM_LIMIT = 58 * 1024 * 1024
INT_MAX = 2**31 - 1

f32 = jnp.float32
bf16 = jnp.bfloat16
i32 = jnp.int32


def _rms(x, g):
    return x * lax.rsqrt(jnp.mean(x * x, axis=-1, keepdims=True) + EPS) * g


def _sigmoid(x):
    return 1.0 / (1.0 + jnp.exp(-x))


def _cparams(sem, limit=VMEM_LIMIT):
    return pltpu.CompilerParams(dimension_semantics=sem, vmem_limit_bytes=limit)


def _in_proj_body(x_ref, g_ref, w_ref, up_ref, v_ref, gate_ref, h_scr, u_even, u_odd, *, nj, last):
    s = pl.program_id(0)
    j = jnp.minimum(s, last) % nj

    @pl.when(s == 0)
    def _():
        u_odd[...] = jnp.zeros_like(u_odd)

    @pl.when((j == 0) & (s <= last))
    def _():
        h_scr[...] = _rms(x_ref[...], g_ref[...]).astype(bf16)

    def step(done_ref, next_ref):
        prev = done_ref[...]
        q = prev.shape[1] // 2
        sig = _sigmoid(prev)
        up_ref[...] = prev
        v_ref[...] = prev[:, :q] * sig[:, q:]
        gate_ref[...] = sig.astype(bf16)
        next_ref[...] = jnp.dot(h_scr[...], w_ref[...], preferred_element_type=f32)

    @pl.when(s % 2 == 0)
    def _():
        step(u_odd, u_even)

    @pl.when(s % 2 == 1)
    def _():
        step(u_even, u_odd)


def _in_proj(x2, g1, w_perm, *, tm, tn):
    n, d = x2.shape
    n_pool, n_conv, n_gate = (d // 2) // tn, d // tn, (2 * d) // tn
    nj = n_pool + n_conv + n_gate
    last = (n // tm) * nj - 1
    q = tn // 2
    body = functools.partial(_in_proj_body, nj=nj, last=last)
    cur = lambda s: jnp.minimum(s, last)
    done = lambda s: jnp.maximum(s - 1, 0)

    def owned(first, count):
        before, after = count, count + (1 if first > 0 else 0)
        def index(s):
            jp = done(s) % nj - first
            return done(s) // nj, jnp.where(jp < 0, before, jnp.where(jp < count, jp, after))
        return index
    return pl.pallas_call(
        body,
        grid=(last + 2,),
        in_specs=[
            pl.BlockSpec((tm, d), lambda s: (cur(s) // nj, 0)),
            pl.BlockSpec((1, d), lambda s: (0, 0)),
            pl.BlockSpec((d, tn), lambda s: (0, cur(s) % nj)),
        ],
        out_specs=[
            pl.BlockSpec((tm, tn), owned(0, n_pool)),
            pl.BlockSpec((tm, q), owned(n_pool, n_conv)),
            pl.BlockSpec((tm, tn), owned(n_pool + n_conv, n_gate)),
        ],
        out_shape=[
            jax.ShapeDtypeStruct((n, d // 2 + tn), f32),
            jax.ShapeDtypeStruct((n, d // 2 + 2 * q), f32),
            jax.ShapeDtypeStruct((n, 2 * d + tn), bf16),
        ],
        scratch_shapes=[pltpu.VMEM((tm, d), bf16), pltpu.VMEM((tm, tn), f32), pltpu.VMEM((tm, tn), f32)],
        compiler_params=_cparams(("arbitrary",)),
        name="in_proj",
    )(x2, g1, w_perm)


def _mixer_body(x_ref, upc_ref, upp_ref, upn_ref, vc_ref, vp_ref, vn_ref, gate_ref,
                wpool_ref, pscale_ref, dww_ref, dwb_ref, cng_ref, cnb_ref, wco_ref, wout_ref,
                n2g_ref, wr_ref, xmid_ref, h2_ref, aff_ref, afft_ref, upext, vext, vshift, act,
                *, t, seq, n_exp, conv_k):
    j = pl.program_id(0)
    d = x_ref.shape[1]

    def edges(tile):
        s0 = (tile * t) % seq
        return s0, s0 > 0, s0 + t < seq

    def conv_branch(c_ref, p_ref, n_ref, tile, act_ref):
        _, has_prev, has_next = edges(tile)
        vext[0:HALO, :] = jnp.where(has_prev, p_ref[...], 0.0)
        vext[HALO:HALO + t, :] = c_ref[...]
        vext[HALO + t:, :] = jnp.where(has_next, n_ref[...], 0.0)
        pad = conv_k // 2
        cw = vext.shape[1]
        sub = vshift.shape[0]
        half = t // 2
        first = HALO - pad
        conv = []
        for c0 in range(0, cw, LANES):
            cols = slice(c0, c0 + LANES)
            for s in range(sub):
                vshift[s] = vext[s:s + vshift.shape[1], cols]
            halves = []
            for r0 in (0, half):
                acc = jnp.zeros((half, LANES), f32) + dwb_ref[:, cols]
                for k in range(conv_k):
                    s, a = (first + k) % sub, (first + k) // sub * sub
                    acc = acc + dww_ref[k:k + 1, cols] * vshift[s, r0 + a:r0 + a + half, :]
                halves.append(acc)
            conv.append(jnp.concatenate(halves, axis=0))
        conv = jnp.concatenate(conv, axis=1)
        mu = jnp.mean(conv, axis=-1, keepdims=True)
        xc = conv - mu
        var = jnp.mean(xc * xc, axis=-1, keepdims=True)
        ln = xc * lax.rsqrt(var + EPS) * cng_ref[...] + cnb_ref[...]
        act_ref[...] = (ln * _sigmoid(ln)).astype(bf16)

    def finish_tile(act_ref):
        s0, has_prev, has_next = edges(j)
        upext[0:HALO, :] = jnp.where(has_prev, upp_ref[...], 0.0)
        upext[HALO:HALO + t, :] = upc_ref[...]
        upext[HALO + t:, :] = jnp.where(has_next, upn_ref[...], 0.0)

        spos = s0 + lax.broadcasted_iota(i32, (t, 1), 0)
        gch = upext.shape[1] // len(POOL_WINDOWS)
        y_pool = []
        for gi, w in enumerate(POOL_WINDOWS):
            left = w // 2
            right = w - 1 - left
            cols = slice(gi * gch, (gi + 1) * gch)
            total = upext[HALO - left:HALO - left + t, cols]
            for dd in range(-left + 1, right + 1):
                total = total + upext[HALO + dd:HALO + dd + t, cols]
            lo = jnp.maximum(spos - left, 0)
            hi = jnp.minimum(spos + right, seq - 1) + 1
            count = (hi - lo).astype(f32)
            mixed = total / count - upext[HALO:HALO + t, cols]
            y_pool.append(jnp.dot(mixed.astype(bf16), wpool_ref[gi], preferred_element_type=f32))
        y_pool = jnp.concatenate(y_pool, axis=1) * pscale_ref[...]

        y_conv = jnp.dot(act_ref[...], wco_ref[...], preferred_element_type=f32)
        merged = (gate_ref[:, :d].astype(f32) * y_pool + gate_ref[:, d:2 * d].astype(f32) * y_conv)
        x_mid = x_ref[...] + jnp.dot(merged.astype(bf16), wout_ref[...], preferred_element_type=f32)
        xmid_ref[...] = x_mid

        h2 = _rms(x_mid, n2g_ref[...])
        h_hi = h2.astype(bf16)
        h2_ref[...] = h_hi
        h_lo = (h2 - h_hi.astype(f32)).astype(bf16)
        both = jnp.dot(h_hi, wr_ref[...], preferred_element_type=f32)
        logits = (both[:, :LANES] + both[:, LANES:]
                  + jnp.dot(h_lo, wr_ref[:, :LANES], preferred_element_type=f32))
        lane = lax.broadcasted_iota(i32, logits.shape, 1)
        logits = jnp.where(lane < n_exp, logits, -1e30)
        ex = jnp.exp(logits - jnp.max(logits, axis=-1, keepdims=True))
        aff = ex / jnp.sum(ex, axis=-1, keepdims=True)
        aff_ref[...] = aff
        afft_ref[...] = aff.T[:n_exp, :]

    conv_branch(vc_ref, vp_ref, vn_ref, j, act)
    finish_tile(act)


def _const_spec(shape):
    zeros = (0,) * len(shape)
    return pl.BlockSpec(shape, lambda j: zeros, pipeline_mode=pl.Buffered(1))


def _mixer(x2, up, v, gates, wpool, pscale, dww, dwb, cng, cnb, wco, wout, n2g, wr_pad,
           *, t, seq, n_exp):
    n, d = x2.shape
    hw = d // 2
    nh = n // HALO
    th = t // HALO
    conv_k = dww.shape[0]
    cur = lambda j: (j, 0)
    prev = lambda j: (jnp.maximum(j * th - 1, 0), 0)
    nxt = lambda j: (jnp.minimum((j + 1) * th, nh - 1), 0)
    body = functools.partial(_mixer_body, t=t, seq=seq, n_exp=n_exp, conv_k=conv_k)
    return pl.pallas_call(
        body,
        grid=(n // t,),
        in_specs=[
            pl.BlockSpec((t, d), cur),
            pl.BlockSpec((t, hw), cur), pl.BlockSpec((HALO, hw), prev), pl.BlockSpec((HALO, hw), nxt),
            pl.BlockSpec((t, hw), cur), pl.BlockSpec((HALO, hw), prev), pl.BlockSpec((HALO, hw), nxt),
            pl.BlockSpec((t, 2 * d), cur),
            _const_spec(wpool.shape), _const_spec(pscale.shape), _const_spec(dww.shape),
            _const_spec(dwb.shape), _const_spec(cng.shape), _const_spec(cnb.shape),
            _const_spec(wco.shape), _const_spec(wout.shape), _const_spec(n2g.shape),
            _const_spec(wr_pad.shape),
        ],
        out_specs=[
            pl.BlockSpec((t, d), cur),
            pl.BlockSpec((t, d), cur),
            pl.BlockSpec((t, LANES), cur),
            pl.BlockSpec((n_exp, t), lambda j: (0, j)),
        ],
        out_shape=[
            jax.ShapeDtypeStruct((n, d), f32),
            jax.ShapeDtypeStruct((n, d), bf16),
            jax.ShapeDtypeStruct((n, LANES), f32),
            jax.ShapeDtypeStruct((n_exp, n), f32),
        ],
        scratch_shapes=[pltpu.VMEM((t + 2 * HALO, hw), f32), pltpu.VMEM((t + 2 * HALO, hw), f32),
                        pltpu.VMEM((F32_ROWS, t + 2 * HALO - F32_ROWS, LANES), f32),
                        pltpu.VMEM((t, hw), bf16)],
        compiler_params=_cparams(("arbitrary",)),
        name="mixer",
    )(x2, up, up, up, v, v, v, gates, wpool, pscale, dww, dwb, cng, cnb, wco, wout, n2g, wr_pad)


def _route_body(afft_ref, thr_ref, need_ref, base_ref, *, cap, t, nt):
    bits = pltpu.bitcast(afft_ref[...], i32)
    n_exp = bits.shape[0]
    ntp = base_ref.shape[1]
    cur = jnp.zeros((n_exp, 1), i32)
    for b in range(30, -1, -1):
        cand = cur | (1 << b)
        cnt = jnp.sum((bits >= cand).astype(f32), axis=1, keepdims=True)
        cur = jnp.where(cnt >= cap, cand, cur)
    gt = (bits > cur).astype(f32)
    eq = (bits == cur).astype(f32)
    need = cap - jnp.sum(gt, axis=1, keepdims=True)

    lane = lax.broadcasted_iota(i32, (n_exp, ntp), 1)
    cgt = jnp.zeros((n_exp, ntp), f32)
    ceq = jnp.zeros((n_exp, ntp), f32)
    for jt in range(nt):
        cols = slice(jt * t, (jt + 1) * t)
        cgt = jnp.where(lane == jt, jnp.sum(gt[:, cols], axis=1, keepdims=True), cgt)
        ceq = jnp.where(lane == jt, jnp.sum(eq[:, cols], axis=1, keepdims=True), ceq)
    r = lax.broadcasted_iota(i32, (ntp, ntp), 0)
    c = lax.broadcasted_iota(i32, (ntp, ntp), 1)
    upper = (r < c).astype(f32)
    excl = functools.partial(jnp.dot, preferred_element_type=f32, precision=lax.Precision.HIGHEST)
    eq_before = excl(ceq, upper)
    csel = cgt + jnp.clip(need - eq_before, 0.0, ceq)
    base_ref[...] = excl(csel, upper).astype(i32)
    thr_ref[...] = jnp.broadcast_to(cur, thr_ref.shape)
    need_ref[...] = jnp.broadcast_to(need.astype(i32), need_ref.shape)


def _route(afft, *, cap, t):
    n_exp, n = afft.shape
    nt = n // t
    ntp = -(-(nt + 1) // LANES) * LANES
    body = functools.partial(_route_body, cap=cap, t=t, nt=nt)
    return pl.pallas_call(
        body,
        out_shape=[
            jax.ShapeDtypeStruct((n_exp, LANES), i32),
            jax.ShapeDtypeStruct((n_exp, LANES), i32),
            jax.ShapeDtypeStruct((n_exp, ntp), i32),
        ],
        compiler_params=pltpu.CompilerParams(vmem_limit_bytes=VMEM_LIMIT),
        name="route",
    )(afft)


def _align_down(x, m):
    return (x // m) * m


def _aligned(start):
    return start if isinstance(start, int) else pl.multiple_of(start, BF16_ROWS)


def _dispatch_copy(stage_ref, xe_hbm, sem, e, start):
    return pltpu.make_async_copy(
        stage_ref, xe_hbm.at[e, pl.ds(_aligned(start), SLOT_BLOCK), :], sem)


def _dispatch_body(base_smem, h2_ref, aff_ref, afft_ref, thr_ref, need_ref, tri_ref, exw_ref, exc_ref, xe_hbm,
                   cnt_sel, cnt_eq, carry, stage, stage_x, sems, sem_x, *, t, nt, n_exp, cap):
    j = pl.program_id(0)
    rb = SLOT_BLOCK
    grp = BF16_ROWS

    @pl.when(j == 0)
    def _():
        cnt_sel[...] = jnp.zeros_like(cnt_sel)
        cnt_eq[...] = jnp.zeros_like(cnt_eq)
        carry[...] = jnp.zeros_like(carry)
        stage_x[...] = jnp.zeros_like(stage_x)
        for e in range(n_exp):
            cp = _dispatch_copy(stage_x, xe_hbm, sem_x.at[0], e, cap)
            cp.start()
            cp.wait()

    aff = aff_ref[...]
    g_hi = aff.astype(bf16)
    rest = aff - g_hi.astype(f32)
    g_mid = rest.astype(bf16)
    g_lo = (rest - g_mid.astype(f32)).astype(bf16)
    h2 = jnp.concatenate([h2_ref[...], g_hi, g_mid, g_lo], axis=1)
    bits = pltpu.bitcast(afft_ref[...], i32)
    thr = thr_ref[:, :1]
    need = need_ref[:, :1]
    gt = bits > thr
    eq = bits == thr
    upper = tri_ref[...]
    eq_f = jnp.where(eq, 1.0, 0.0)
    eq_rank = jnp.dot(eq_f.astype(bf16), upper, preferred_element_type=f32).astype(i32)
    ce = cnt_eq[:, :1]
    sel = gt | (eq & (ce + eq_rank < need))
    sel_f = jnp.where(sel, 1.0, 0.0)
    pos = jnp.dot(sel_f.astype(bf16), upper, preferred_element_type=f32).astype(i32)
    cs = cnt_sel[:, :1]
    slot = cs + pos
    end = cs + jnp.sum(sel_f, axis=1, keepdims=True).astype(i32)
    cnt_sel[...] = jnp.broadcast_to(end, cnt_sel.shape)
    cnt_eq[...] = jnp.broadcast_to(ce + jnp.sum(eq_f, axis=1, keepdims=True).astype(i32), cnt_eq.shape)
    start = _align_down(cs, grp)
    nstart = _align_down(end, grp)
    rel = slot - start
    relc = slot - nstart
    fill = jnp.full((LANES - n_exp, t), -1.0, f32)

    def onehot(relv, rows, ex_ref):
        relv = jnp.where(sel & (relv >= 0) & (relv < rows), relv, -1).astype(f32)
        relv = jnp.concatenate([relv, fill], axis=0).astype(bf16)
        spread = jnp.dot(ex_ref[...], relv, preferred_element_type=f32)
        rr = lax.broadcasted_iota(i32, spread.shape, 0) % rows
        return jnp.where(spread.astype(i32) == rr, 1.0, 0.0).astype(bf16)

    oh = jnp.concatenate([onehot(rel, rb, exw_ref), onehot(relc, grp, exc_ref)], axis=0)
    res = jnp.dot(oh, h2, preferred_element_type=f32)

    nblk = []
    for e in range(n_exp):
        b0 = base_smem[e, j]
        st = _align_down(b0, grp)
        b1 = base_smem[e, j + 1]
        cnt = b1 - b0
        nblk.append(jnp.where(cnt > 0, (b0 - st + cnt + rb - 1) // rb, 0))

        @pl.when(j > 0)
        def _():
            _dispatch_copy(stage.at[e], xe_hbm, sems.at[e], e, 0).wait()

        old = carry[e]
        stage[e, 0:grp, :] = (res[e * rb:e * rb + grp] + old).astype(bf16)
        stage[e, grp:, :] = res[e * rb + grp:(e + 1) * rb].astype(bf16)
        _dispatch_copy(stage.at[e], xe_hbm, sems.at[e], e, st).start()
        part = res[n_exp * rb + e * grp:n_exp * rb + (e + 1) * grp]
        carry[e] = jnp.where(_align_down(b1, grp) == st, old + part, part)

    rounds = functools.reduce(jnp.maximum, nblk)

    def extra_round(b, _):
        res_b = jnp.dot(onehot(rel - b * rb, rb, exw_ref), h2, preferred_element_type=f32)
        for e in range(n_exp):
            @pl.when(b < nblk[e])
            def _():
                st = _align_down(base_smem[e, j], grp) + b * rb
                stage_x[...] = res_b[e * rb:(e + 1) * rb].astype(bf16)
                cp = _dispatch_copy(stage_x, xe_hbm, sem_x.at[0], e, st)
                cp.start()
                cp.wait()
        return 0

    lax.fori_loop(1, rounds, extra_round, 0)

    @pl.when(j == nt - 1)
    def _():
        for e in range(n_exp):
            _dispatch_copy(stage.at[e], xe_hbm, sems.at[e], e, 0).wait()


def _dispatch(base, h2, aff, afft, thr, need, tri, exw, exc, *, t, cap):
    n, d = h2.shape
    width = d + GATE_COLS
    n_exp = afft.shape[0]
    nt = n // t
    body = functools.partial(_dispatch_body, t=t, nt=nt, n_exp=n_exp, cap=cap)
    grid_spec = pltpu.PrefetchScalarGridSpec(
        num_scalar_prefetch=1,
        grid=(nt,),
        in_specs=[
            pl.BlockSpec((t, d), lambda j, b: (j, 0)),
            pl.BlockSpec((t, LANES), lambda j, b: (j, 0)),
            pl.BlockSpec((n_exp, t), lambda j, b: (0, j)),
            pl.BlockSpec((n_exp, LANES), lambda j, b: (0, 0)),
            pl.BlockSpec((n_exp, LANES), lambda j, b: (0, 0)),
            pl.BlockSpec((t, t), lambda j, b: (0, 0)),
            pl.BlockSpec(exw.shape, lambda j, b: (0, 0)),
            pl.BlockSpec(exc.shape, lambda j, b: (0, 0)),
        ],
        out_specs=pl.BlockSpec(memory_space=pl.ANY),
        scratch_shapes=[
            pltpu.VMEM((n_exp, LANES), i32),
            pltpu.VMEM((n_exp, LANES), i32),
            pltpu.VMEM((n_exp, BF16_ROWS, width), f32),
            pltpu.VMEM((n_exp, SLOT_BLOCK, width), bf16),
            pltpu.VMEM((SLOT_BLOCK, width), bf16),
            pltpu.SemaphoreType.DMA((n_exp,)),
            pltpu.SemaphoreType.DMA((1,)),
        ],
    )
    return pl.pallas_call(
        body,
        grid_spec=grid_spec,
        out_shape=jax.ShapeDtypeStruct((n_exp, cap + SLOT_BLOCK, width), bf16),
        compiler_params=_cparams(("arbitrary",)),
        name="dispatch",
    )(base, h2, aff, afft, thr, need, tri, exw, exc)


def _ffn_up_body(xe_ref, wg_ref, wu_ref, hid_ref, wg_s, wu_s):
    @pl.when(pl.program_id(2) == 0)
    def _():
        wg_s[...] = wg_ref[...].astype(bf16)
        wu_s[...] = wu_ref[...].astype(bf16)

    x = xe_ref[...]
    g = jnp.dot(x, wg_s[...], preferred_element_type=f32)
    u = jnp.dot(x, wu_s[...], preferred_element_type=f32)
    hid_ref[...] = (g * _sigmoid(g) * u).astype(bf16)


def _ffn_up(xe, w_gate, w_up, layer, *, cap, tm, tn):
    n_exp = xe.shape[0]
    d, ff = w_gate.shape[-2:]
    wspec = pl.BlockSpec((None, None, d, tn), lambda e, jf, i: (layer, e, 0, jf))
    return pl.pallas_call(
        _ffn_up_body,
        grid=(n_exp, ff // tn, cap // tm),
        in_specs=[pl.BlockSpec((None, tm, d), lambda e, jf, i: (e, i, 0)), wspec, wspec],
        out_specs=pl.BlockSpec((None, tm, tn), lambda e, jf, i: (e, i, jf)),
        out_shape=jax.ShapeDtypeStruct((n_exp, cap, ff), bf16),
        scratch_shapes=[pltpu.VMEM((d, tn), bf16), pltpu.VMEM((d, tn), bf16)],
        compiler_params=_cparams(("arbitrary", "arbitrary", "arbitrary")),
        name="ffn_up",
    )(xe, w_gate, w_up)


def _ffn_down_body(hid_ref, ghi_ref, gmid_ref, glo_ref, wd_ref, ye_ref, wd_s):
    @pl.when(pl.program_id(2) == 0)
    def _():
        wd_s[...] = wd_ref[...].astype(bf16)

    pieces = ghi_ref[...].astype(f32) + gmid_ref[...].astype(f32) + glo_ref[...].astype(f32)
    lane = lax.broadcasted_iota(i32, pieces.shape, 1)
    gate = jnp.sum(jnp.where(lane == pl.program_id(0), pieces, 0.0), axis=1, keepdims=True)
    y = jnp.dot(hid_ref[...], wd_s[...], preferred_element_type=f32)
    ye_ref[...] = (y * gate).astype(bf16)


def _ffn_down(hid, xe, w_down, layer, *, tm, tn):
    n_exp, cap, ff = hid.shape
    d = w_down.shape[-1]
    gcol = d // LANES
    gspec = lambda k: pl.BlockSpec((None, tm, LANES), lambda e, jd, i: (e, i, gcol + k))
    return pl.pallas_call(
        _ffn_down_body,
        grid=(n_exp, d // tn, cap // tm),
        in_specs=[
            pl.BlockSpec((None, tm, ff), lambda e, jd, i: (e, i, 0)),
            gspec(0), gspec(1), gspec(2),
            pl.BlockSpec((None, None, ff, tn), lambda e, jd, i: (layer, e, 0, jd)),
        ],
        out_specs=pl.BlockSpec((None, tm, tn), lambda e, jd, i: (e, i, jd)),
        out_shape=jax.ShapeDtypeStruct((n_exp, cap, d), bf16),
        scratch_shapes=[pltpu.VMEM((ff, tn), bf16)],
        compiler_params=_cparams(("arbitrary", "arbitrary", "arbitrary")),
        name="ffn_down",
    )(hid, xe, xe, xe, w_down)


def _combine_copy(ye_hbm, win, sems, buf, e, start):
    return pltpu.make_async_copy(
        ye_hbm.at[e, pl.ds(_aligned(start), SLOT_BLOCK), :],
        win.at[buf, pl.ds(e * SLOT_BLOCK, SLOT_BLOCK), :], sems.at[buf, e])


def _combine_body(base_smem, x_ref, aff_ref, thr_ref, need_ref, fg_ref, tri_ref, exp_ref, ye_hbm, out_ref,
                  cnt_sel, cnt_eq, win, sems, *, t, nt, cap, n_exp, final):
    j = pl.program_id(0)
    rb = SLOT_BLOCK
    grp = BF16_ROWS
    last = cap - rb
    buf = j % 2

    def start_windows(step, into):
        for e in range(n_exp):
            st = _align_down(base_smem[e, step], grp)
            _combine_copy(ye_hbm, win, sems, into, e, jnp.minimum(st, last)).start()

    @pl.when(j == 0)
    def _():
        start_windows(0, 0)

    @pl.when(j + 1 < nt)
    def _():
        start_windows(j + 1, 1 - buf)

    @pl.when(j == 0)
    def _():
        cnt_sel[...] = jnp.zeros_like(cnt_sel)
        cnt_eq[...] = jnp.zeros_like(cnt_eq)

    nblk = []
    for e in range(n_exp):
        b0 = base_smem[e, j]
        st = _align_down(b0, grp)
        cnt = base_smem[e, j + 1] - b0
        nblk.append(jnp.where(cnt > 0, (b0 - st + cnt + rb - 1) // rb, 0))
    rounds = functools.reduce(jnp.maximum, nblk)

    aff = aff_ref[...]
    bits = pltpu.bitcast(aff, i32)
    thr = thr_ref[:1, :]
    need = need_ref[:1, :]
    gt = bits > thr
    eq = bits == thr
    lower = tri_ref[...]
    eq_f = jnp.where(eq, 1.0, 0.0)
    eq_rank = jnp.dot(lower, eq_f.astype(bf16), preferred_element_type=f32).astype(i32)
    ce = cnt_eq[:1, :]
    sel = gt | (eq & (ce + eq_rank < need))
    sel_f = jnp.where(sel, 1.0, 0.0)
    pos = jnp.dot(lower, sel_f.astype(bf16), preferred_element_type=f32).astype(i32)
    cs = cnt_sel[:1, :]
    slot = cs + pos
    cnt_sel[...] = jnp.broadcast_to(cs + jnp.sum(sel_f, axis=0, keepdims=True).astype(i32), cnt_sel.shape)
    cnt_eq[...] = jnp.broadcast_to(ce + jnp.sum(eq_f, axis=0, keepdims=True).astype(i32), cnt_eq.shape)
    start = _align_down(cs, grp)
    lane_row = lax.broadcasted_iota(i32, (t, n_exp * rb), 1) % rb

    def onehots(b):
        first = start + b * rb
        rel = slot - jnp.minimum(first, last)
        rel = jnp.where(sel & (slot >= first) & (rel < rb), rel, -1)
        spread = jnp.dot(rel.astype(f32).astype(bf16), exp_ref[...], preferred_element_type=f32)
        return jnp.where(spread.astype(i32) == lane_row, 1.0, 0.0).astype(bf16)

    oh = onehots(0)
    for e in range(n_exp):
        _combine_copy(ye_hbm, win, sems, buf, e, 0).wait()
    acc = x_ref[...] + jnp.dot(oh, win[buf], preferred_element_type=f32)

    def extra_round(b, acc):
        for e in range(n_exp):
            @pl.when(b < nblk[e])
            def _():
                st = _align_down(base_smem[e, j], grp) + b * rb
                cp = _combine_copy(ye_hbm, win, sems, buf, e, jnp.minimum(st, last))
                cp.start()
                cp.wait()
        return acc + jnp.dot(onehots(b), win[buf], preferred_element_type=f32)

    acc = lax.fori_loop(1, rounds, extra_round, acc)
    if final:
        acc = _rms(acc, fg_ref[...])
    out_ref[...] = acc


def _combine(base, x_mid, aff, thr_row, need_row, final_g, tri, expand, ye, *, t, cap, final):
    n, d = x_mid.shape
    n_exp = ye.shape[0]
    body = functools.partial(_combine_body, t=t, nt=n // t, cap=cap, n_exp=n_exp, final=final)
    grid_spec = pltpu.PrefetchScalarGridSpec(
        num_scalar_prefetch=1,
        grid=(n // t,),
        in_specs=[
            pl.BlockSpec((t, d), lambda j, b: (j, 0)),
            pl.BlockSpec((t, LANES), lambda j, b: (j, 0)),
            pl.BlockSpec((8, LANES), lambda j, b: (0, 0)),
            pl.BlockSpec((8, LANES), lambda j, b: (0, 0)),
            pl.BlockSpec((1, d), lambda j, b: (0, 0)),
            pl.BlockSpec((t, t), lambda j, b: (0, 0)),
            pl.BlockSpec(expand.shape, lambda j, b: (0, 0)),
            pl.BlockSpec(memory_space=pl.ANY),
        ],
        out_specs=pl.BlockSpec((t, d), lambda j, b: (j, 0)),
        scratch_shapes=[
            pltpu.VMEM((8, LANES), i32),
            pltpu.VMEM((8, LANES), i32),
            pltpu.VMEM((2, n_exp * SLOT_BLOCK, d), bf16),
            pltpu.SemaphoreType.DMA((2, n_exp)),
        ],
    )
    return pl.pallas_call(
        body,
        grid_spec=grid_spec,
        out_shape=jax.ShapeDtypeStruct((n, d), f32),
        compiler_params=_cparams(("arbitrary",)),
        name="combine",
    )(base, x_mid, aff, thr_row, need_row, final_g, tri, expand, ye)


def _pick(n, options):
    for o in options:
        if n % o == 0:
            return o
    raise ValueError(f"no tile in {options} divides {n}")


def _prep_layer_weights(w_in, w_pool, w_conv_out, w_out, w_router):
    d = w_in.shape[1]
    tn = d // 4
    q = tn // 2
    hw = d // 2
    a = w_in[:, :, hw:d]
    g = w_in[:, :, d:d + hw]
    conv_cols = []
    for c0 in range(0, hw, q):
        conv_cols += [a[:, :, c0:c0 + q], g[:, :, c0:c0 + q]]
    w_perm = jnp.concatenate([w_in[:, :, :hw]] + conv_cols + [w_in[:, :, d + hw:]], axis=2)
    n_exp = w_router.shape[-1]
    wr_f32 = jnp.pad(w_router, ((0, 0), (0, 0), (0, LANES - n_exp)))
    wr_hi = wr_f32.astype(bf16)
    wr_lo = (wr_f32 - wr_hi.astype(f32)).astype(bf16)
    wr_pad = jnp.concatenate([wr_hi, wr_lo], axis=2)
    return (w_perm.astype(bf16), w_pool.astype(bf16), w_conv_out.astype(bf16),
            w_out.astype(bf16), wr_pad, tn)


def _run_trunk(x, params, prepped):
    (norm1_g, pool_scale, dw_w, dw_b, cn_g, cn_b, norm2_g, w_gate, w_up, w_down, final_g) = params
    w_perm, w_pool, w_conv_out, w_out, wr_pad, tn_in = prepped
    b, seq, d = x.shape
    n = b * seq
    depth = norm1_g.shape[0]
    n_exp = w_gate.shape[1]
    ff = w_gate.shape[-1]
    cap = max(1, CAPACITY_FACTOR * n // n_exp)
    t = TOKEN_TILE
    assert seq % t == 0 and cap % SLOT_BLOCK == 0 and cap >= SLOT_BLOCK and n_exp <= LANES
    tm_in = _pick(n, (1024, 512, 256))
    tm_ffn = _pick(cap, (1024, 512, 256, 128))
    tn_ffn = _pick(ff, (512, 256, 128))
    tn_down = _pick(d, (1024, 512, 256, 128))
    row = lambda a: a.reshape(1, -1)
    upper = jnp.triu(jnp.ones((t, t), bf16), 1)
    lower = jnp.tril(jnp.ones((t, t), bf16), -1)
    spread = lambda rows: (jnp.arange(LANES)[:, None] == jnp.arange(n_exp * rows)[None, :] // rows).astype(bf16)
    expand = spread(SLOT_BLOCK)
    exw, exc = expand.T, spread(BF16_ROWS).T

    x2 = x.reshape(n, d)
    for l in range(depth):
        up, v, gates = _in_proj(x2, row(norm1_g[l]), w_perm[l], tm=tm_in, tn=tn_in)
        x_mid, h2, aff, afft = _mixer(
            x2, up, v, gates, w_pool[l], row(pool_scale[l]), dw_w[l], row(dw_b[l]), row(cn_g[l]),
            row(cn_b[l]), w_conv_out[l], w_out[l], row(norm2_g[l]), wr_pad[l],
            t=t, seq=seq, n_exp=n_exp)
        thr, need, base = _route(afft, cap=cap, t=t)
        thr_row = jnp.full((8, LANES), INT_MAX, i32).at[:, :n_exp].set(thr[:, 0][None, :])
        need_row = jnp.zeros((8, LANES), i32).at[:, :n_exp].set(need[:, 0][None, :])
        xe = _dispatch(base, h2, aff, afft, thr, need, upper, exw, exc, t=t, cap=cap)
        hid = _ffn_up(xe, w_gate, w_up, l, cap=cap, tm=tm_ffn, tn=tn_ffn)
        ye = _ffn_down(hid, xe, w_down, l, tm=tm_ffn, tn=tn_down)
        x2 = _combine(base, x_mid, aff, thr_row, need_row, row(final_g), lower, expand, ye,
                      t=t, cap=cap, final=(l == depth - 1))
    return x2.reshape(b, seq, d)


def kernel(x_prompt, x_sample, norm1_g, w_in, w_pool, pool_scale, dw_w, dw_b, cn_g, cn_b,
           w_conv_out, w_out, norm2_g, w_router, w_gate, w_up, w_down, final_g):
    prepped = _prep_layer_weights(w_in, w_pool, w_conv_out, w_out, w_router)
    params = (norm1_g, pool_scale, dw_w, dw_b, cn_g, cn_b, norm2_g, w_gate, w_up, w_down, final_g)
    return (_run_trunk(x_prompt, params, prepped), _run_trunk(x_sample, params, prepped))
```

```python
import functools

import jax
import jax.numpy as jnp
from jax import lax
from jax.experimental import pallas as pl
from jax.experimental.pallas import tpu as pltpu

EPS = 1e-6
POOL_WINDOWS = (2, 4, 8, 16)
CAPACITY_FACTOR = 2

LANES = 128
F32_ROWS = 8
BF16_ROWS = 16
HALO = 16
SLOT_BLOCK = 64
GATE_COLS = 3 * LANES
TOKEN_TILE = 256
VMEM_LIMIT = 58 * 1024 * 1024
INT_MAX = 2**31 - 1

f32 = jnp.float32
bf16 = jnp.bfloat16
i32 = jnp.int32


def _rms(x, g):
    return x * lax.rsqrt(jnp.mean(x * x, axis=-1, keepdims=True) + EPS) * g


def _sigmoid(x):
    return 1.0 / (1.0 + jnp.exp(-x))


def _cparams(sem, limit=VMEM_LIMIT):
    return pltpu.CompilerParams(dimension_semantics=sem, vmem_limit_bytes=limit)


def _in_proj_body(x_ref, g_ref, w_ref, up_ref, v_ref, gate_ref, h_scr, u_even, u_odd, *, nj, last):
    s = pl.program_id(0)
    j = jnp.minimum(s, last) % nj

    @pl.when(s == 0)
    def _():
        u_odd[...] = jnp.zeros_like(u_odd)

    @pl.when((j == 0) & (s <= last))
    def _():
        h_scr[...] = _rms(x_ref[...], g_ref[...]).astype(bf16)

    def step(done_ref, next_ref):
        prev = done_ref[...]
        q = prev.shape[1] // 2
        sig = _sigmoid(prev)
        up_ref[...] = prev
        v_ref[...] = prev[:, :q] * sig[:, q:]
        gate_ref[...] = sig.astype(bf16)
        next_ref[...] = jnp.dot(h_scr[...], w_ref[...], preferred_element_type=f32)

    @pl.when(s % 2 == 0)
    def _():
        step(u_odd, u_even)

    @pl.when(s % 2 == 1)
    def _():
        step(u_even, u_odd)


def _in_proj(x2, g1, w_perm, *, tm, tn):
    n, d = x2.shape
    n_pool, n_conv, n_gate = (d // 2) // tn, d // tn, (2 * d) // tn
    nj = n_pool + n_conv + n_gate
    last = (n // tm) * nj - 1
    q = tn // 2
    body = functools.partial(_in_proj_body, nj=nj, last=last)
    cur = lambda s: jnp.minimum(s, last)
    done = lambda s: jnp.maximum(s - 1, 0)

    def owned(first, count):
        before, after = count, count + (1 if first > 0 else 0)
        def index(s):
            jp = done(s) % nj - first
            return done(s) // nj, jnp.where(jp < 0, before, jnp.where(jp < count, jp, after))
        return index
    return pl.pallas_call(
        body,
        grid=(last + 2,),
        in_specs=[
            pl.BlockSpec((tm, d), lambda s: (cur(s) // nj, 0)),
            pl.BlockSpec((1, d), lambda s: (0, 0)),
            pl.BlockSpec((None, d, tn), lambda s: (cur(s) % nj, 0, 0)),
        ],
        out_specs=[
            pl.BlockSpec((tm, tn), owned(0, n_pool)),
            pl.BlockSpec((tm, q), owned(n_pool, n_conv)),
            pl.BlockSpec((tm, tn), owned(n_pool + n_conv, n_gate)),
        ],
        out_shape=[
            jax.ShapeDtypeStruct((n, d // 2 + tn), f32),
            jax.ShapeDtypeStruct((n, d // 2 + 2 * q), f32),
            jax.ShapeDtypeStruct((n, 2 * d + tn), bf16),
        ],
        scratch_shapes=[pltpu.VMEM((tm, d), bf16), pltpu.VMEM((tm, tn), f32), pltpu.VMEM((tm, tn), f32)],
        compiler_params=_cparams(("arbitrary",)),
        name="in_proj",
    )(x2, g1, w_perm)


def _mixer_body(x_ref, upc_ref, upp_ref, upn_ref, vc_ref, vp_ref, vn_ref, gate_ref,
                wpool_ref, pscale_ref, dww_ref, dwb_ref, cng_ref, cnb_ref, wco_ref, wout_ref,
                n2g_ref, wr_ref, xmid_ref, h2_ref, aff_ref, afft_ref, upext, vext, vshift, act, conv_scr,
                *, t, seq, n_exp, conv_k):
    j = pl.program_id(0)
    d = x_ref.shape[1]

    def edges(tile):
        s0 = (tile * t) % seq
        return s0, s0 > 0, s0 + t < seq

    def conv_branch(c_ref, p_ref, n_ref, tile, act_ref):
        _, has_prev, has_next = edges(tile)
        n_chunks = vext.shape[0]
        for c in range(n_chunks):
            cols = slice(c * LANES, (c + 1) * LANES)
            vext[c, 0:HALO, :] = jnp.where(has_prev, p_ref[:, cols], 0.0)
            vext[c, HALO:HALO + t, :] = c_ref[:, cols]
            vext[c, HALO + t:, :] = jnp.where(has_next, n_ref[:, cols], 0.0)
        pad = conv_k // 2
        sub = vshift.shape[0]
        half = t // 2
        first = HALO - pad

        def chunk(c, carry):
            for s in range(sub):
                vshift[s] = vext[c, s:s + vshift.shape[1], :]
            for r0 in (0, half):
                acc = jnp.zeros((half, LANES), f32) + dwb_ref[c]
                for k in range(conv_k):
                    s, a = (first + k) % sub, (first + k) // sub * sub
                    acc = acc + dww_ref[c, k:k + 1, :] * vshift[s, r0 + a:r0 + a + half, :]
                conv_scr[c, r0:r0 + half, :] = acc
            return carry

        lax.fori_loop(0, n_chunks, chunk, 0)
        conv = jnp.concatenate([conv_scr[c] for c in range(n_chunks)], axis=1)
        mu = jnp.mean(conv, axis=-1, keepdims=True)
        xc = conv - mu
        var = jnp.mean(xc * xc, axis=-1, keepdims=True)
        ln = xc * lax.rsqrt(var + EPS) * cng_ref[...] + cnb_ref[...]
        act_ref[...] = (ln * _sigmoid(ln)).astype(bf16)

    def finish_tile(act_ref):
        s0, has_prev, has_next = edges(j)
        upext[0:HALO, :] = jnp.where(has_prev, upp_ref[...], 0.0)
        upext[HALO:HALO + t, :] = upc_ref[...]
        upext[HALO + t:, :] = jnp.where(has_next, upn_ref[...], 0.0)

        spos = s0 + lax.broadcasted_iota(i32, (t, 1), 0)
        gch = upext.shape[1] // len(POOL_WINDOWS)
        y_pool = []
        for gi, w in enumerate(POOL_WINDOWS):
            left = w // 2
            right = w - 1 - left
            cols = slice(gi * gch, (gi + 1) * gch)
            total = upext[HALO - left:HALO - left + t, cols]
            for dd in range(-left + 1, right + 1):
                total = total + upext[HALO + dd:HALO + dd + t, cols]
            lo = jnp.maximum(spos - left, 0)
            hi = jnp.minimum(spos + right, seq - 1) + 1
            count = (hi - lo).astype(f32)
            mixed = total / count - upext[HALO:HALO + t, cols]
            y_pool.append(jnp.dot(mixed.astype(bf16), wpool_ref[gi], preferred_element_type=f32))
        y_pool = jnp.concatenate(y_pool, axis=1) * pscale_ref[...]

        y_conv = jnp.dot(act_ref[...], wco_ref[...], preferred_element_type=f32)
        merged = (gate_ref[:, :d].astype(f32) * y_pool + gate_ref[:, d:2 * d].astype(f32) * y_conv)
        x_mid = x_ref[...] + jnp.dot(merged.astype(bf16), wout_ref[...], preferred_element_type=f32)
        xmid_ref[...] = x_mid

        h2 = _rms(x_mid, n2g_ref[...])
        h_hi = h2.astype(bf16)
        h2_ref[...] = h_hi
        h_lo = (h2 - h_hi.astype(f32)).astype(bf16)
        both = jnp.dot(h_hi, wr_ref[...], preferred_element_type=f32)
        logits = (both[:, :LANES] + both[:, LANES:]
                  + jnp.dot(h_lo, wr_ref[:, :LANES], preferred_element_type=f32))
        lane = lax.broadcasted_iota(i32, logits.shape, 1)
        logits = jnp.where(lane < n_exp, logits, -1e30)
        ex = jnp.exp(logits - jnp.max(logits, axis=-1, keepdims=True))
        aff = ex / jnp.sum(ex, axis=-1, keepdims=True)
        aff_ref[...] = aff
        afft_ref[...] = aff.T[:n_exp, :]

    conv_branch(vc_ref, vp_ref, vn_ref, j, act)
    finish_tile(act)


def _const_spec(shape):
    zeros = (0,) * len(shape)
    return pl.BlockSpec(shape, lambda j: zeros, pipeline_mode=pl.Buffered(1))


def _mixer(x2, up, v, gates, wpool, pscale, dww, dwb, cng, cnb, wco, wout, n2g, wr_pad,
           *, t, seq, n_exp):
    n, d = x2.shape
    hw = d // 2
    nh = n // HALO
    th = t // HALO
    conv_k = dww.shape[1]
    cur = lambda j: (j, 0)
    prev = lambda j: (jnp.maximum(j * th - 1, 0), 0)
    nxt = lambda j: (jnp.minimum((j + 1) * th, nh - 1), 0)
    body = functools.partial(_mixer_body, t=t, seq=seq, n_exp=n_exp, conv_k=conv_k)
    return pl.pallas_call(
        body,
        grid=(n // t,),
        in_specs=[
            pl.BlockSpec((t, d), cur),
            pl.BlockSpec((t, hw), cur), pl.BlockSpec((HALO, hw), prev), pl.BlockSpec((HALO, hw), nxt),
            pl.BlockSpec((t, hw), cur), pl.BlockSpec((HALO, hw), prev), pl.BlockSpec((HALO, hw), nxt),
            pl.BlockSpec((t, 2 * d), cur),
            _const_spec(wpool.shape), _const_spec(pscale.shape), _const_spec(dww.shape),
            _const_spec(dwb.shape), _const_spec(cng.shape), _const_spec(cnb.shape),
            _const_spec(wco.shape), _const_spec(wout.shape), _const_spec(n2g.shape),
            _const_spec(wr_pad.shape),
        ],
        out_specs=[
            pl.BlockSpec((t, d), cur),
            pl.BlockSpec((t, d), cur),
            pl.BlockSpec((t, LANES), cur),
            pl.BlockSpec((n_exp, t), lambda j: (0, j)),
        ],
        out_shape=[
            jax.ShapeDtypeStruct((n, d), f32),
            jax.ShapeDtypeStruct((n, d), bf16),
            jax.ShapeDtypeStruct((n, LANES), f32),
            jax.ShapeDtypeStruct((n_exp, n), f32),
        ],
        scratch_shapes=[pltpu.VMEM((t + 2 * HALO, hw), f32), pltpu.VMEM((hw // LANES, t + 2 * HALO, LANES), f32),
                        pltpu.VMEM((F32_ROWS, t + 2 * HALO - F32_ROWS, LANES), f32),
                        pltpu.VMEM((t, hw), bf16), pltpu.VMEM((hw // LANES, t, LANES), f32)],
        compiler_params=_cparams(("arbitrary",)),
        name="mixer",
    )(x2, up, up, up, v, v, v, gates, wpool, pscale, dww, dwb, cng, cnb, wco, wout, n2g, wr_pad)


def _route_body(afft_ref, thr_ref, need_ref, base_ref, *, cap, t, nt):
    bits = pltpu.bitcast(afft_ref[...], i32)
    n_exp = bits.shape[0]
    ntp = base_ref.shape[1]
    cur = jnp.zeros((n_exp, 1), i32)
    for b in range(30, -1, -1):
        cand = cur | (1 << b)
        cnt = jnp.sum((bits >= cand).astype(f32), axis=1, keepdims=True)
        cur = jnp.where(cnt >= cap, cand, cur)
    gt = (bits > cur).astype(f32)
    eq = (bits == cur).astype(f32)
    need = cap - jnp.sum(gt, axis=1, keepdims=True)

    lane = lax.broadcasted_iota(i32, (n_exp, ntp), 1)
    cgt = jnp.zeros((n_exp, ntp), f32)
    ceq = jnp.zeros((n_exp, ntp), f32)
    for jt in range(nt):
        cols = slice(jt * t, (jt + 1) * t)
        cgt = jnp.where(lane == jt, jnp.sum(gt[:, cols], axis=1, keepdims=True), cgt)
        ceq = jnp.where(lane == jt, jnp.sum(eq[:, cols], axis=1, keepdims=True), ceq)
    r = lax.broadcasted_iota(i32, (ntp, ntp), 0)
    c = lax.broadcasted_iota(i32, (ntp, ntp), 1)
    upper = (r < c).astype(f32)
    excl = functools.partial(jnp.dot, preferred_element_type=f32, precision=lax.Precision.HIGHEST)
    eq_before = excl(ceq, upper)
    csel = cgt + jnp.clip(need - eq_before, 0.0, ceq)
    base_ref[...] = excl(csel, upper).astype(i32)
    thr_ref[...] = jnp.broadcast_to(cur, thr_ref.shape)
    need_ref[...] = jnp.broadcast_to(need.astype(i32), need_ref.shape)


def _route(afft, *, cap, t):
    n_exp, n = afft.shape
    nt = n // t
    ntp = -(-(nt + 1) // LANES) * LANES
    body = functools.partial(_route_body, cap=cap, t=t, nt=nt)
    return pl.pallas_call(
        body,
        out_shape=[
            jax.ShapeDtypeStruct((n_exp, LANES), i32),
            jax.ShapeDtypeStruct((n_exp, LANES), i32),
            jax.ShapeDtypeStruct((n_exp, ntp), i32),
        ],
        compiler_params=pltpu.CompilerParams(vmem_limit_bytes=VMEM_LIMIT),
        name="route",
    )(afft)


def _align_down(x, m):
    return (x // m) * m


def _aligned(start):
    return start if isinstance(start, int) else pl.multiple_of(start, BF16_ROWS)


def _dispatch_copy(stage_ref, xe_hbm, sem, e, start):
    return pltpu.make_async_copy(
        stage_ref, xe_hbm.at[e, pl.ds(_aligned(start), SLOT_BLOCK), :], sem)


def _dispatch_body(base_smem, h2_ref, aff_ref, afft_ref, thr_ref, need_ref, tri_ref, exw_ref, exc_ref, xe_hbm,
                   cnt_sel, cnt_eq, carry, stage, stage_x, sems, sem_x, *, t, nt, n_exp, cap):
    j = pl.program_id(0)
    rb = SLOT_BLOCK
    grp = BF16_ROWS

    @pl.when(j == 0)
    def _():
        cnt_sel[...] = jnp.zeros_like(cnt_sel)
        cnt_eq[...] = jnp.zeros_like(cnt_eq)
        carry[...] = jnp.zeros_like(carry)
        stage_x[...] = jnp.zeros_like(stage_x)
        for e in range(n_exp):
            cp = _dispatch_copy(stage_x, xe_hbm, sem_x.at[0], e, cap)
            cp.start()
            cp.wait()

    aff = aff_ref[...]
    g_hi = aff.astype(bf16)
    rest = aff - g_hi.astype(f32)
    g_mid = rest.astype(bf16)
    g_lo = (rest - g_mid.astype(f32)).astype(bf16)
    h2 = jnp.concatenate([h2_ref[...], g_hi, g_mid, g_lo], axis=1)
    bits = pltpu.bitcast(afft_ref[...], i32)
    thr = thr_ref[:, :1]
    need = need_ref[:, :1]
    gt = bits > thr
    eq = bits == thr
    upper = tri_ref[...]
    eq_f = jnp.where(eq, 1.0, 0.0)
    eq_rank = jnp.dot(eq_f.astype(bf16), upper, preferred_element_type=f32).astype(i32)
    ce = cnt_eq[:, :1]
    sel = gt | (eq & (ce + eq_rank < need))
    sel_f = jnp.where(sel, 1.0, 0.0)
    pos = jnp.dot(sel_f.astype(bf16), upper, preferred_element_type=f32).astype(i32)
    cs = cnt_sel[:, :1]
    slot = cs + pos
    end = cs + jnp.sum(sel_f, axis=1, keepdims=True).astype(i32)
    cnt_sel[...] = jnp.broadcast_to(end, cnt_sel.shape)
    cnt_eq[...] = jnp.broadcast_to(ce + jnp.sum(eq_f, axis=1, keepdims=True).astype(i32), cnt_eq.shape)
    start = _align_down(cs, grp)
    nstart = _align_down(end, grp)
    rel = slot - start
    relc = slot - nstart
    fill = jnp.full((LANES - n_exp, t), -1.0, f32)

    def onehot(relv, rows, ex_ref):
        relv = jnp.where(sel & (relv >= 0) & (relv < rows), relv, -1).astype(f32)
        relv = jnp.concatenate([relv, fill], axis=0).astype(bf16)
        spread = jnp.dot(ex_ref[...], relv, preferred_element_type=f32)
        rr = lax.broadcasted_iota(i32, spread.shape, 0) % rows
        return jnp.where(spread.astype(i32) == rr, 1.0, 0.0).astype(bf16)

    oh = jnp.concatenate([onehot(rel, rb, exw_ref), onehot(relc, grp, exc_ref)], axis=0)
    res = jnp.dot(oh, h2, preferred_element_type=f32)

    nblk = []
    for e in range(n_exp):
        b0 = base_smem[e, j]
        st = _align_down(b0, grp)
        b1 = base_smem[e, j + 1]
        cnt = b1 - b0
        nblk.append(jnp.where(cnt > 0, (b0 - st + cnt + rb - 1) // rb, 0))

        @pl.when(j > 0)
        def _():
            _dispatch_copy(stage.at[e], xe_hbm, sems.at[e], e, 0).wait()

        old = carry[e]
        stage[e, 0:grp, :] = (res[e * rb:e * rb + grp] + old).astype(bf16)
        stage[e, grp:, :] = res[e * rb + grp:(e + 1) * rb].astype(bf16)
        _dispatch_copy(stage.at[e], xe_hbm, sems.at[e], e, st).start()
        part = res[n_exp * rb + e * grp:n_exp * rb + (e + 1) * grp]
        carry[e] = jnp.where(_align_down(b1, grp) == st, old + part, part)

    rounds = functools.reduce(jnp.maximum, nblk)

    def extra_round(b, _):
        res_b = jnp.dot(onehot(rel - b * rb, rb, exw_ref), h2, preferred_element_type=f32)
        for e in range(n_exp):
            @pl.when(b < nblk[e])
            def _():
                st = _align_down(base_smem[e, j], grp) + b * rb
                stage_x[...] = res_b[e * rb:(e + 1) * rb].astype(bf16)
                cp = _dispatch_copy(stage_x, xe_hbm, sem_x.at[0], e, st)
                cp.start()
                cp.wait()
        return 0

    lax.fori_loop(1, rounds, extra_round, 0)

    @pl.when(j == nt - 1)
    def _():
        for e in range(n_exp):
            _dispatch_copy(stage.at[e], xe_hbm, sems.at[e], e, 0).wait()


def _dispatch(base, h2, aff, afft, thr, need, tri, exw, exc, *, t, cap):
    n, d = h2.shape
    width = d + GATE_COLS
    n_exp = afft.shape[0]
    nt = n // t
    body = functools.partial(_dispatch_body, t=t, nt=nt, n_exp=n_exp, cap=cap)
    grid_spec = pltpu.PrefetchScalarGridSpec(
        num_scalar_prefetch=1,
        grid=(nt,),
        in_specs=[
            pl.BlockSpec((t, d), lambda j, b: (j, 0)),
            pl.BlockSpec((t, LANES), lambda j, b: (j, 0)),
            pl.BlockSpec((n_exp, t), lambda j, b: (0, j)),
            pl.BlockSpec((n_exp, LANES), lambda j, b: (0, 0)),
            pl.BlockSpec((n_exp, LANES), lambda j, b: (0, 0)),
            pl.BlockSpec((t, t), lambda j, b: (0, 0)),
            pl.BlockSpec(exw.shape, lambda j, b: (0, 0)),
            pl.BlockSpec(exc.shape, lambda j, b: (0, 0)),
        ],
        out_specs=pl.BlockSpec(memory_space=pl.ANY),
        scratch_shapes=[
            pltpu.VMEM((n_exp, LANES), i32),
            pltpu.VMEM((n_exp, LANES), i32),
            pltpu.VMEM((n_exp, BF16_ROWS, width), f32),
            pltpu.VMEM((n_exp, SLOT_BLOCK, width), bf16),
            pltpu.VMEM((SLOT_BLOCK, width), bf16),
            pltpu.SemaphoreType.DMA((n_exp,)),
            pltpu.SemaphoreType.DMA((1,)),
        ],
    )
    return pl.pallas_call(
        body,
        grid_spec=grid_spec,
        out_shape=jax.ShapeDtypeStruct((n_exp, cap + SLOT_BLOCK, width), bf16),
        compiler_params=_cparams(("arbitrary",)),
        name="dispatch",
    )(base, h2, aff, afft, thr, need, tri, exw, exc)


def _ffn_up_body(xe_ref, wg_ref, wu_ref, hid_ref, wg_s, wu_s):
    @pl.when(pl.program_id(2) == 0)
    def _():
        wg_s[...] = wg_ref[...].astype(bf16)
        wu_s[...] = wu_ref[...].astype(bf16)

    x = xe_ref[...]
    g = jnp.dot(x, wg_s[...], preferred_element_type=f32)
    u = jnp.dot(x, wu_s[...], preferred_element_type=f32)
    hid_ref[...] = (g * _sigmoid(g) * u).astype(bf16)


def _ffn_up(xe, w_gate, w_up, layer, *, cap, tm, tn):
    n_exp = xe.shape[0]
    d, ff = w_gate.shape[-2:]
    wspec = pl.BlockSpec((None, None, d, tn), lambda e, jf, i: (layer, e, 0, jf))
    return pl.pallas_call(
        _ffn_up_body,
        grid=(n_exp, ff // tn, cap // tm),
        in_specs=[pl.BlockSpec((None, tm, d), lambda e, jf, i: (e, i, 0)), wspec, wspec],
        out_specs=pl.BlockSpec((None, tm, tn), lambda e, jf, i: (e, i, jf)),
        out_shape=jax.ShapeDtypeStruct((n_exp, cap, ff), bf16),
        scratch_shapes=[pltpu.VMEM((d, tn), bf16), pltpu.VMEM((d, tn), bf16)],
        compiler_params=_cparams(("arbitrary", "arbitrary", "arbitrary")),
        name="ffn_up",
    )(xe, w_gate, w_up)


def _ffn_down_body(hid_ref, ghi_ref, gmid_ref, glo_ref, wd_ref, ye_ref, wd_s):
    @pl.when(pl.program_id(2) == 0)
    def _():
        wd_s[...] = wd_ref[...].astype(bf16)

    pieces = ghi_ref[...].astype(f32) + gmid_ref[...].astype(f32) + glo_ref[...].astype(f32)
    lane = lax.broadcasted_iota(i32, pieces.shape, 1)
    gate = jnp.sum(jnp.where(lane == pl.program_id(0), pieces, 0.0), axis=1, keepdims=True)
    y = jnp.dot(hid_ref[...], wd_s[...], preferred_element_type=f32)
    ye_ref[...] = (y * gate).astype(bf16)


def _ffn_down(hid, xe, w_down, layer, *, tm, tn):
    n_exp, cap, ff = hid.shape
    d = w_down.shape[-1]
    gcol = d // LANES
    gspec = lambda k: pl.BlockSpec((None, tm, LANES), lambda e, jd, i: (e, i, gcol + k))
    return pl.pallas_call(
        _ffn_down_body,
        grid=(n_exp, d // tn, cap // tm),
        in_specs=[
            pl.BlockSpec((None, tm, ff), lambda e, jd, i: (e, i, 0)),
            gspec(0), gspec(1), gspec(2),
            pl.BlockSpec((None, None, ff, tn), lambda e, jd, i: (layer, e, 0, jd)),
        ],
        out_specs=pl.BlockSpec((None, tm, tn), lambda e, jd, i: (e, i, jd)),
        out_shape=jax.ShapeDtypeStruct((n_exp, cap, d), bf16),
        scratch_shapes=[pltpu.VMEM((ff, tn), bf16)],
        compiler_params=_cparams(("arbitrary", "arbitrary", "arbitrary")),
        name="ffn_down",
    )(hid, xe, xe, xe, w_down)


def _combine_copy(ye_hbm, win, sems, buf, e, start):
    return pltpu.make_async_copy(
        ye_hbm.at[e, pl.ds(_aligned(start), SLOT_BLOCK), :],
        win.at[buf, pl.ds(e * SLOT_BLOCK, SLOT_BLOCK), :], sems.at[buf, e])


def _combine_body(base_smem, x_ref, aff_ref, thr_ref, need_ref, fg_ref, tri_ref, exp_ref, ye_hbm, out_ref,
                  cnt_sel, cnt_eq, win, sems, *, t, nt, cap, n_exp, final):
    j = pl.program_id(0)
    rb = SLOT_BLOCK
    grp = BF16_ROWS
    last = cap - rb
    buf = j % 2

    def start_windows(step, into):
        for e in range(n_exp):
            st = _align_down(base_smem[e, step], grp)
            _combine_copy(ye_hbm, win, sems, into, e, jnp.minimum(st, last)).start()

    @pl.when(j == 0)
    def _():
        start_windows(0, 0)

    @pl.when(j + 1 < nt)
    def _():
        start_windows(j + 1, 1 - buf)

    @pl.when(j == 0)
    def _():
        cnt_sel[...] = jnp.zeros_like(cnt_sel)
        cnt_eq[...] = jnp.zeros_like(cnt_eq)

    nblk = []
    for e in range(n_exp):
        b0 = base_smem[e, j]
        st = _align_down(b0, grp)
        cnt = base_smem[e, j + 1] - b0
        nblk.append(jnp.where(cnt > 0, (b0 - st + cnt + rb - 1) // rb, 0))
    rounds = functools.reduce(jnp.maximum, nblk)

    aff = aff_ref[...]
    bits = pltpu.bitcast(aff, i32)
    thr = thr_ref[:1, :]
    need = need_ref[:1, :]
    gt = bits > thr
    eq = bits == thr
    lower = tri_ref[...]
    eq_f = jnp.where(eq, 1.0, 0.0)
    eq_rank = jnp.dot(lower, eq_f.astype(bf16), preferred_element_type=f32).astype(i32)
    ce = cnt_eq[:1, :]
    sel = gt | (eq & (ce + eq_rank < need))
    sel_f = jnp.where(sel, 1.0, 0.0)
    pos = jnp.dot(lower, sel_f.astype(bf16), preferred_element_type=f32).astype(i32)
    cs = cnt_sel[:1, :]
    slot = cs + pos
    cnt_sel[...] = jnp.broadcast_to(cs + jnp.sum(sel_f, axis=0, keepdims=True).astype(i32), cnt_sel.shape)
    cnt_eq[...] = jnp.broadcast_to(ce + jnp.sum(eq_f, axis=0, keepdims=True).astype(i32), cnt_eq.shape)
    start = _align_down(cs, grp)
    lane_row = lax.broadcasted_iota(i32, (t, n_exp * rb), 1) % rb

    def onehots(b):
        first = start + b * rb
        rel = slot - jnp.minimum(first, last)
        rel = jnp.where(sel & (slot >= first) & (rel < rb), rel, -1)
        spread = jnp.dot(rel.astype(f32).astype(bf16), exp_ref[...], preferred_element_type=f32)
        return jnp.where(spread.astype(i32) == lane_row, 1.0, 0.0).astype(bf16)

    oh = onehots(0)
    for e in range(n_exp):
        _combine_copy(ye_hbm, win, sems, buf, e, 0).wait()
    acc = x_ref[...] + jnp.dot(oh, win[buf], preferred_element_type=f32)

    def extra_round(b, acc):
        for e in range(n_exp):
            @pl.when(b < nblk[e])
            def _():
                st = _align_down(base_smem[e, j], grp) + b * rb
                cp = _combine_copy(ye_hbm, win, sems, buf, e, jnp.minimum(st, last))
                cp.start()
                cp.wait()
        return acc + jnp.dot(onehots(b), win[buf], preferred_element_type=f32)

    acc = lax.fori_loop(1, rounds, extra_round, acc)
    if final:
        acc = _rms(acc, fg_ref[...])
    out_ref[...] = acc


def _combine(base, x_mid, aff, thr_row, need_row, final_g, tri, expand, ye, *, t, cap, final):
    n, d = x_mid.shape
    n_exp = ye.shape[0]
    body = functools.partial(_combine_body, t=t, nt=n // t, cap=cap, n_exp=n_exp, final=final)
    grid_spec = pltpu.PrefetchScalarGridSpec(
        num_scalar_prefetch=1,
        grid=(n // t,),
        in_specs=[
            pl.BlockSpec((t, d), lambda j, b: (j, 0)),
            pl.BlockSpec((t, LANES), lambda j, b: (j, 0)),
            pl.BlockSpec((8, LANES), lambda j, b: (0, 0)),
            pl.BlockSpec((8, LANES), lambda j, b: (0, 0)),
            pl.BlockSpec((1, d), lambda j, b: (0, 0)),
            pl.BlockSpec((t, t), lambda j, b: (0, 0)),
            pl.BlockSpec(expand.shape, lambda j, b: (0, 0)),
            pl.BlockSpec(memory_space=pl.ANY),
        ],
        out_specs=pl.BlockSpec((t, d), lambda j, b: (j, 0)),
        scratch_shapes=[
            pltpu.VMEM((8, LANES), i32),
            pltpu.VMEM((8, LANES), i32),
            pltpu.VMEM((2, n_exp * SLOT_BLOCK, d), bf16),
            pltpu.SemaphoreType.DMA((2, n_exp)),
        ],
    )
    return pl.pallas_call(
        body,
        grid_spec=grid_spec,
        out_shape=jax.ShapeDtypeStruct((n, d), f32),
        compiler_params=_cparams(("arbitrary",)),
        name="combine",
    )(base, x_mid, aff, thr_row, need_row, final_g, tri, expand, ye)


def _pick(n, options):
    for o in options:
        if n % o == 0:
            return o
    raise ValueError(f"no tile in {options} divides {n}")


def _prep_layer_weights(w_in, w_pool, w_conv_out, w_out, w_router):
    d = w_in.shape[1]
    tn = d // 4
    q = tn // 2
    hw = d // 2
    a = w_in[:, :, hw:d]
    g = w_in[:, :, d:d + hw]
    conv_cols = []
    for c0 in range(0, hw, q):
        conv_cols += [a[:, :, c0:c0 + q], g[:, :, c0:c0 + q]]
    w_perm = jnp.concatenate([w_in[:, :, :hw]] + conv_cols + [w_in[:, :, d + hw:]], axis=2)
    n_exp = w_router.shape[-1]
    wr_f32 = jnp.pad(w_router, ((0, 0), (0, 0), (0, LANES - n_exp)))
    wr_hi = wr_f32.astype(bf16)
    wr_lo = (wr_f32 - wr_hi.astype(f32)).astype(bf16)
    wr_pad = jnp.concatenate([wr_hi, wr_lo], axis=2)
    w_blocks = w_perm.astype(bf16).reshape(w_in.shape[0], d, -1, tn).transpose(0, 2, 1, 3)
    return (w_blocks, w_pool.astype(bf16), w_conv_out.astype(bf16),
            w_out.astype(bf16), wr_pad, tn)


def _run_trunk(x, params, prepped):
    (norm1_g, pool_scale, dw_w, dw_b, cn_g, cn_b, norm2_g, w_gate, w_up, w_down, final_g) = params
    w_perm, w_pool, w_conv_out, w_out, wr_pad, tn_in = prepped
    b, seq, d = x.shape
    n = b * seq
    depth = norm1_g.shape[0]
    n_exp = w_gate.shape[1]
    ff = w_gate.shape[-1]
    cap = max(1, CAPACITY_FACTOR * n // n_exp)
    t = TOKEN_TILE
    assert seq % t == 0 and cap % SLOT_BLOCK == 0 and cap >= SLOT_BLOCK and n_exp <= LANES
    tm_in = _pick(n, (1024, 512, 256))
    tm_ffn = _pick(cap, (1024, 512, 256, 128))
    tn_ffn = _pick(ff, (512, 256, 128))
    tn_down = _pick(d, (1024, 512, 256, 128))
    row = lambda a: a.reshape(1, -1)
    chunked = lambda a: a.reshape(a.shape[0], -1, LANES).transpose(1, 0, 2)
    upper = jnp.triu(jnp.ones((t, t), bf16), 1)
    lower = jnp.tril(jnp.ones((t, t), bf16), -1)
    spread = lambda rows: (jnp.arange(LANES)[:, None] == jnp.arange(n_exp * rows)[None, :] // rows).astype(bf16)
    expand = spread(SLOT_BLOCK)
    exw, exc = expand.T, spread(BF16_ROWS).T

    x2 = x.reshape(n, d)
    for l in range(depth):
        up, v, gates = _in_proj(x2, row(norm1_g[l]), w_perm[l], tm=tm_in, tn=tn_in)
        x_mid, h2, aff, afft = _mixer(
            x2, up, v, gates, w_pool[l], row(pool_scale[l]), chunked(dw_w[l]), chunked(row(dw_b[l])), row(cn_g[l]),
            row(cn_b[l]), w_conv_out[l], w_out[l], row(norm2_g[l]), wr_pad[l],
            t=t, seq=seq, n_exp=n_exp)
        thr, need, base = _route(afft, cap=cap, t=t)
        thr_row = jnp.full((8, LANES), INT_MAX, i32).at[:, :n_exp].set(thr[:, 0][None, :])
        need_row = jnp.zeros((8, LANES), i32).at[:, :n_exp].set(need[:, 0][None, :])
        xe = _dispatch(base, h2, aff, afft, thr, need, upper, exw, exc, t=t, cap=cap)
        hid = _ffn_up(xe, w_gate, w_up, l, cap=cap, tm=tm_ffn, tn=tn_ffn)
        ye = _ffn_down(hid, xe, w_down, l, tm=tm_ffn, tn=tn_down)
        x2 = _combine(base, x_mid, aff, thr_row, need_row, row(final_g), lower, expand, ye,
                      t=t, cap=cap, final=(l == depth - 1))
    return x2.reshape(b, seq, d)


def kernel(x_prompt, x_sample, norm1_g, w_in, w_pool, pool_scale, dw_w, dw_b, cn_g, cn_b,
           w_conv_out, w_out, norm2_g, w_router, w_gate, w_up, w_down, final_g):
    prepped = _prep_layer_weights(w_in, w_pool, w_conv_out, w_out, w_router)
    params = (norm1_g, pool_scale, dw_w, dw_b, cn_g, cn_b, norm2_g, w_gate, w_up, w_down, final_g)
    return (_run_trunk(x_prompt, params, prepped), _run_trunk(x_sample, params, prepped))
```
